```python
import math
import jax
import jax.numpy as jnp
from jax import lax
import numpy as np

D_MODEL = 4096
BATCH = 4
SEQ = 4096
DEPTH = 1

EPS = 1e-6
MLA_HEADS = 16
Q_LORA = 1024
KV_LORA = 512
QK_NOPE = 128
QK_ROPE = 64
V_HEAD = 128
ROPE_THETA = 10000.0
Q_BLOCK = 128
GDN_HEADS = 16
GDN_DK = 128
GDN_DV = 128
CONV_W = 5
CHUNK = 64
N_EXPERTS = 32
TOP_K = 4
D_FF = D_MODEL // 2
SWIGLU_LIMIT = 7.0
SWIGLU_ALPHA = 1.702
MOE_BLOCK = 128

MLA_WIDTH = MLA_HEADS * V_HEAD
GDN_WIDTH = GDN_HEADS * GDN_DV
GDN_QKV = 2 * GDN_HEADS * GDN_DK + GDN_HEADS * GDN_DV
IN_SIZES = (Q_LORA, KV_LORA, QK_ROPE, GDN_QKV, GDN_WIDTH, 4 * GDN_HEADS, D_MODEL, D_MODEL)
IN_DIM = sum(IN_SIZES)

kernel_name = "hybrid_mla_gdn_moe_encoder"


def _rmsnorm(x, g):
    xf = x.astype(jnp.float32)
    y = xf * lax.rsqrt(jnp.mean(xf * xf, axis=-1, keepdims=True) + EPS)
    return (y * g.astype(jnp.float32)).astype(x.dtype)


def _l2norm(x):
    return x * lax.rsqrt(jnp.sum(x * x, axis=-1, keepdims=True) + EPS)


def _rope(t, cos, sin):
    half = t.shape[-1] // 2
    t1, t2 = t[..., :half], t[..., half:]
    return jnp.concatenate([t1 * cos - t2 * sin, t2 * cos + t1 * sin], axis=-1)


def _mla(c_q, c_kv, k_rope_raw, positions, q_norm_g, w_q_up, kv_norm_g, w_kv_up):
    B, S, _ = c_q.shape
    dt = c_q.dtype
    q = (_rmsnorm(c_q, q_norm_g) @ w_q_up).reshape(B, S, MLA_HEADS, QK_NOPE + QK_ROPE)
    kv = (_rmsnorm(c_kv, kv_norm_g) @ w_kv_up).reshape(B, S, MLA_HEADS, QK_NOPE + V_HEAD)
    q_nope, q_rope = q[..., :QK_NOPE], q[..., QK_NOPE:]
    k_nope, v = kv[..., :QK_NOPE], kv[..., QK_NOPE:]
    inv_freq = ROPE_THETA ** (-jnp.arange(0, QK_ROPE, 2, dtype=jnp.float32) / QK_ROPE)
    ang = positions.astype(jnp.float32)[..., None] * inv_freq
    cos, sin = jnp.cos(ang).astype(dt), jnp.sin(ang).astype(dt)
    q_rope = _rope(q_rope, cos[:, :, None], sin[:, :, None])
    k_rope = _rope(k_rope_raw, cos, sin)
    scale = (QK_NOPE + QK_ROPE) ** -0.5
    nb = S // Q_BLOCK
    qn_blocks = q_nope.reshape(B, nb, Q_BLOCK, MLA_HEADS, QK_NOPE).swapaxes(0, 1)
    qr_blocks = q_rope.reshape(B, nb, Q_BLOCK, MLA_HEADS, QK_ROPE).swapaxes(0, 1)

    def attend(blk):
        qn, qr = blk
        s = jnp.einsum('bqhd,bkhd->bhqk', qn, k_nope) + jnp.einsum('bqhr,bkr->bhqk', qr, k_rope)
        p = jax.nn.softmax(s.astype(jnp.float32) * scale, axis=-1).astype(dt)
        return jnp.einsum('bhqk,bkhd->bqhd', p, v)

    o = lax.map(attend, (qn_blocks, qr_blocks))
    return o.swapaxes(0, 1).reshape(B, S, MLA_WIDTH)


def _short_conv(u, w):
    C = u.shape[-1]
    pad = CONV_W // 2
    y = lax.conv_general_dilated(u, w[:, None, :].astype(u.dtype), window_strides=(1,),
                                 padding=[(pad, pad)], dimension_numbers=('NWC', 'WIO', 'NWC'),
                                 feature_group_count=C)
    return jax.nn.silu(y)


def _gated_delta_chunked(q, k, v, g, beta):
    B, H, S, Dk = q.shape
    Dv = v.shape[-1]
    n = S // CHUNK
    q, k, v = (t.reshape(B, H, n, CHUNK, t.shape[-1]) for t in (q, k, v))
    g, beta = (t.reshape(B, H, n, CHUNK) for t in (g, beta))
    G = jnp.cumsum(g, axis=-1)
    incl = jnp.tril(jnp.ones((CHUNK, CHUNK), dtype=bool))
    strict = jnp.tril(jnp.ones((CHUNK, CHUNK), dtype=bool), k=-1)
    decay = jnp.exp(jnp.where(incl, G[..., :, None] - G[..., None, :], -jnp.inf))
    kb = k * beta[..., None]
    a = jnp.where(strict, jnp.einsum('bhnid,bhnjd->bhnij', kb, k) * decay, 0.0)
    eye = jnp.eye(CHUNK, dtype=q.dtype)
    rhs = jnp.concatenate([v * beta[..., None], kb * jnp.exp(G)[..., None]], axis=-1)
    sol = lax.linalg.triangular_solve(a + eye, rhs, left_side=True, lower=True, unit_diagonal=True)
    u, w = sol[..., :Dv], sol[..., Dv:]
    qk = jnp.einsum('bhnid,bhnjd->bhnij', q, k) * decay
    q_dec = q * jnp.exp(G)[..., None]
    g_last = G[..., -1]
    k_end = k * jnp.exp(g_last[..., None] - G)[..., None]

    def step(state, xs):
        q_c, qk_c, u_c, w_c, k_c, gl_c = xs
        v_new = u_c - jnp.einsum('bhid,bhdv->bhiv', w_c, state)
        o = jnp.einsum('bhid,bhdv->bhiv', q_c, state) + jnp.einsum('bhij,bhjv->bhiv', qk_c, v_new)
        state = state * jnp.exp(gl_c)[..., None, None] + jnp.einsum('bhid,bhiv->bhdv', k_c, v_new)
        return state, o

    xs = tuple(jnp.moveaxis(t, 2, 0) for t in (q_dec, qk, u, w, k_end, g_last))
    _, o = lax.scan(step, jnp.zeros((B, H, Dk, Dv), q.dtype), xs)
    return jnp.moveaxis(o, 0, 2).reshape(B, H, S, Dv)


def _bi_gdn(qkv_raw, gate_raw, ab_raw, conv_w, a_log, dt_bias, norm_g):
    B, S, _ = qkv_raw.shape
    dt = qkv_raw.dtype
    qkv = _short_conv(qkv_raw, conv_w).astype(jnp.float32)
    kd = GDN_HEADS * GDN_DK

    def to_heads(t, d):
        return t.reshape(B, S, GDN_HEADS, d).transpose(0, 2, 1, 3)

    q = _l2norm(to_heads(qkv[..., :kd], GDN_DK)) * (GDN_DK ** -0.5)
    k = _l2norm(to_heads(qkv[..., kd:2 * kd], GDN_DK))
    v = to_heads(qkv[..., 2 * kd:], GDN_DV)
    ab = ab_raw.astype(jnp.float32).reshape(B, S, 4, GDN_HEADS).transpose(2, 0, 3, 1)
    a_log = a_log.astype(jnp.float32)
    dt_bias = dt_bias.astype(jnp.float32)

    def log_decay(a, d):
        return -jnp.exp(a_log[d])[None, :, None] * jax.nn.softplus(a + dt_bias[d][None, :, None])

    g_f, g_b = log_decay(ab[0], 0), log_decay(ab[1], 1)
    beta_f, beta_b = jax.nn.sigmoid(ab[2]), jax.nn.sigmoid(ab[3])
    o_f = _gated_delta_chunked(q, k, v, g_f, beta_f)

    def flip(t):
        return jnp.flip(t, axis=2)

    o_b = flip(_gated_delta_chunked(flip(q), flip(k), flip(v), flip(g_b), flip(beta_b)))
    o = (o_f + o_b).transpose(0, 2, 1, 3)
    gate = gate_raw.astype(jnp.float32).reshape(B, S, GDN_HEADS, GDN_DV)
    o = _rmsnorm(o, norm_g) * jax.nn.silu(gate)
    return o.reshape(B, S, GDN_WIDTH).astype(dt)


def _moe(h, router_w, router_b, w_gu, b_gu, w_down, b_down):
    B, S, D = h.shape
    T = B * S
    TK = T * TOP_K
    xt = h.reshape(T, D)
    logits = xt.astype(jnp.float32) @ router_w.astype(jnp.float32) + router_b.astype(jnp.float32)
    top_val, top_idx = lax.top_k(logits, TOP_K)
    gates = jax.nn.softmax(top_val, axis=-1)
    flat_e = top_idx.reshape(TK)
    flat_tok = jnp.repeat(jnp.arange(T, dtype=jnp.int32), TOP_K)
    order = jnp.argsort(flat_e)
    e_sorted = flat_e[order]
    tok_sorted = flat_tok[order]
    gate_sorted = gates.reshape(TK)[order]
    counts = jnp.bincount(flat_e, length=N_EXPERTS)
    padded = (counts + MOE_BLOCK - 1) // MOE_BLOCK * MOE_BLOCK
    start = jnp.cumsum(counts) - counts
    pad_end = jnp.cumsum(padded)
    pad_start = pad_end - padded
    dest = pad_start[e_sorted] + jnp.arange(TK, dtype=jnp.int32) - start[e_sorted]
    n_blocks = -(-TK // MOE_BLOCK) + N_EXPERTS
    rows = n_blocks * MOE_BLOCK
    row_tok = jnp.zeros((rows,), jnp.int32).at[dest].set(tok_sorted)
    row_gate = jnp.zeros((rows,), jnp.float32).at[dest].set(gate_sorted)
    block_expert = jnp.minimum(
        jnp.searchsorted(pad_end, jnp.arange(n_blocks, dtype=pad_end.dtype) * MOE_BLOCK, side='right'),
        N_EXPERTS - 1)

    def step(acc, blk):
        tok, gt, e = blk
        gu = xt[tok] @ w_gu[e] + b_gu[e]
        gate, up = gu[:, :D_FF], gu[:, D_FF:]
        gate = jnp.minimum(gate, SWIGLU_LIMIT)
        up = jnp.clip(up, -SWIGLU_LIMIT, SWIGLU_LIMIT)
        act = (up + 1.0) * gate * jax.nn.sigmoid(SWIGLU_ALPHA * gate)
        y = (act @ w_down[e] + b_down[e]) * gt[:, None].astype(xt.dtype)
        return acc.at[tok].add(y), None

    out, _ = lax.scan(step, jnp.zeros_like(xt),
                      (row_tok.reshape(n_blocks, MOE_BLOCK), row_gate.reshape(n_blocks, MOE_BLOCK), block_expert))
    return out.reshape(B, S, D)


def setup_inputs(seed: int = 0) -> dict:
    key = jax.random.key(seed)
    ks = jax.random.split(key, 24)
    f32 = jnp.float32
    L = DEPTH

    def nrm(k, shape, fan_in):
        return jax.random.normal(k, shape, f32) * (fan_in ** -0.5)

    def gain(k, shape):
        return 1.0 + 0.02 * jax.random.normal(k, shape, f32)

    x = jax.random.normal(ks[0], (BATCH, SEQ, D_MODEL), f32)
    offsets = jax.random.randint(ks[1], (BATCH, 1), 0, 1024, dtype=jnp.int32)
    positions = jnp.arange(SEQ, dtype=jnp.int32)[None, :] + offsets
    a_log = jnp.log(jax.random.uniform(ks[9], (L, 2, GDN_HEADS), f32, 1.0, 16.0))
    dt0 = jnp.exp(jax.random.uniform(ks[10], (L, 2, GDN_HEADS), f32, math.log(1e-3), math.log(1e-1)))
    dt_bias = dt0 + jnp.log(-jnp.expm1(-dt0))
    return {
        "x": x,
        "positions": positions,
        "ln1_g": gain(ks[2], (L, D_MODEL)),
        "w_in": nrm(ks[3], (L, D_MODEL, IN_DIM), D_MODEL),
        "q_norm_g": gain(ks[4], (L, Q_LORA)),
        "w_q_up": nrm(ks[5], (L, Q_LORA, MLA_HEADS * (QK_NOPE + QK_ROPE)), Q_LORA),
        "kv_norm_g": gain(ks[6], (L, KV_LORA)),
        "w_kv_up": nrm(ks[7], (L, KV_LORA, MLA_HEADS * (QK_NOPE + V_HEAD)), KV_LORA),
        "gdn_conv_w": nrm(ks[8], (L, CONV_W, GDN_QKV), CONV_W),
        "gdn_a_log": a_log,
        "gdn_dt_bias": dt_bias,
        "gdn_norm_g": gain(ks[11], (L, GDN_DV)),
        "w_mla_o": nrm(ks[12], (L, MLA_WIDTH, D_MODEL), MLA_WIDTH),
        "w_gdn_o": nrm(ks[13], (L, GDN_WIDTH, D_MODEL), GDN_WIDTH),
        "w_out": nrm(ks[14], (L, D_MODEL, D_MODEL), D_MODEL),
        "ln2_g": gain(ks[15], (L, D_MODEL)),
        "router_w": nrm(ks[16], (L, D_MODEL, N_EXPERTS), D_MODEL),
        "router_b": 0.01 * jax.random.normal(ks[17], (L, N_EXPERTS), f32),
        "w_gu": nrm(ks[18], (L, N_EXPERTS, D_MODEL, 2 * D_FF), D_MODEL),
        "b_gu": 0.01 * jax.random.normal(ks[19], (L, N_EXPERTS, 2 * D_FF), f32),
        "w_down": nrm(ks[20], (L, N_EXPERTS, D_FF, D_MODEL), D_FF),
        "b_down": 0.01 * jax.random.normal(ks[21], (L, N_EXPERTS, D_MODEL), f32),
        "ln_f_g": gain(ks[22], (D_MODEL,)),
    }


def reference(x, positions, ln1_g, w_in, q_norm_g, w_q_up, kv_norm_g, w_kv_up, gdn_conv_w, gdn_a_log,
              gdn_dt_bias, gdn_norm_g, w_mla_o, w_gdn_o, w_out, ln2_g, router_w, router_b, w_gu, b_gu,
              w_down, b_down, ln_f_g):
    split_at = np.cumsum(IN_SIZES)[:-1].tolist()
    h = x
    for l in range(DEPTH):
        n = _rmsnorm(h, ln1_g[l])
        c_q, c_kv, k_rope_raw, qkv_raw, gate_raw, ab_raw, z_a, z_b = jnp.split(n @ w_in[l], split_at, axis=-1)
        y_a = _mla(c_q, c_kv, k_rope_raw, positions, q_norm_g[l], w_q_up[l], kv_norm_g[l], w_kv_up[l]) @ w_mla_o[l]
        y_b = _bi_gdn(qkv_raw, gate_raw, ab_raw, gdn_conv_w[l], gdn_a_log[l], gdn_dt_bias[l],
                      gdn_norm_g[l]) @ w_gdn_o[l]
        merged = jax.nn.sigmoid(z_a) * y_a + jax.nn.sigmoid(z_b) * y_b
        h = h + merged @ w_out[l]
        h = h + _moe(_rmsnorm(h, ln2_g[l]), router_w[l], router_b[l], w_gu[l], b_gu[l], w_down[l], b_down[l])
    return _rmsnorm(h, ln_f_g)
```

```python
import functools

import jax
import jax.numpy as jnp
from jax import lax
from jax.experimental import pallas as pl
from jax.experimental.pallas import tpu as pltpu

F32 = jnp.float32
BF16 = jnp.bfloat16

EPS = 1e-6
MLA_HEADS = 16
Q_LORA = 1024
KV_LORA = 512
QK_NOPE = 128
QK_ROPE = 64
V_HEAD = 128
ROPE_THETA = 10000.0
GDN_HEADS = 16
GDN_DK = 128
GDN_DV = 128
CONV_W = 5
CHUNK = 64
N_EXPERTS = 32
TOP_K = 4
SWIGLU_LIMIT = 7.0
SWIGLU_ALPHA = 1.702

LANES = 128
V7X_VMEM_BYTES = 64 * 1024 * 1024
VMEM_CAP = V7X_VMEM_BYTES - 8 * 1024 * 1024
NORM_ROWS = 32
GDN_HG = 2
MOE_BM = 256
HIGHEST = lax.Precision.HIGHEST


def _nbytes(shape, dtype):
    n = 1
    for s in shape:
        n *= s
    return n * jnp.dtype(dtype).itemsize


def _params(sem, blocks, scratch=0):
    need = 2 * sum(blocks) + scratch + 16 * 1024 * 1024
    return pltpu.CompilerParams(dimension_semantics=sem, vmem_limit_bytes=int(min(need, VMEM_CAP)))


def _mm(a, b):
    return jnp.dot(a, b, preferred_element_type=F32)


def _mm_nt(a, b):
    return lax.dot_general(a, b, (((1,), (1,)), ((), ())), preferred_element_type=F32)


def _mm_tn(a, b):
    return lax.dot_general(a, b, (((0,), (0,)), ((), ())), preferred_element_type=F32)


def _norm_rows_into(x_ref, g_ref, xn_ref, rows):
    def body(r, c):
        sl = pl.ds(pl.multiple_of(r * NORM_ROWS, NORM_ROWS), NORM_ROWS)
        xv = x_ref[sl, :].astype(F32)
        ms = jnp.mean(xv * xv, axis=-1, keepdims=True)
        xn_ref[sl, :] = (xv * lax.rsqrt(ms + EPS) * g_ref[...]).astype(BF16)
        return c

    lax.fori_loop(0, rows // NORM_ROWS, body, 0)


def _norm_matmul_kernel(x_ref, g_ref, w_ref, o_ref, xn_ref, *, tm):
    @pl.when(pl.program_id(1) == 0)
    def _():
        _norm_rows_into(x_ref, g_ref, xn_ref, tm)

    o_ref[...] = _mm(xn_ref[...], w_ref[...]).astype(o_ref.dtype)


def norm_matmul(x, g, w, *, out_dtype, tm, tn, x_col_block=0, name):
    T = x.shape[0]
    K, N = w.shape
    tm = min(tm, T)
    assert T % tm == 0 and N % tn == 0 and tm % NORM_ROWS == 0
    blocks = [_nbytes((tm, K), x.dtype), _nbytes((K, tn), BF16), _nbytes((tm, tn), out_dtype)]
    return pl.pallas_call(
        functools.partial(_norm_matmul_kernel, tm=tm),
        grid=(T // tm, N // tn),
        in_specs=[
            pl.BlockSpec((tm, K), lambda i, j: (i, x_col_block)),
            pl.BlockSpec((1, K), lambda i, j: (0, 0)),
            pl.BlockSpec((K, tn), lambda i, j: (0, j)),
        ],
        out_specs=pl.BlockSpec((tm, tn), lambda i, j: (i, j)),
        out_shape=jax.ShapeDtypeStruct((T, N), out_dtype),
        scratch_shapes=[pltpu.VMEM((tm, K), BF16)],
        compiler_params=_params(("parallel", "arbitrary"), blocks, _nbytes((tm, K), BF16)),
        name=name,
    )(x, g, w)


def _gated_pair_kernel(a_ref, wa_ref, za_ref, b_ref, wb_ref, zb_ref, o_ref):
    ya = _mm(a_ref[...], wa_ref[...])
    yb = _mm(b_ref[...], wb_ref[...])
    ga = jax.nn.sigmoid(za_ref[...].astype(F32))
    gb = jax.nn.sigmoid(zb_ref[...].astype(F32))
    o_ref[...] = (ga * ya + gb * yb).astype(o_ref.dtype)


def gated_pair_matmul(a, wa, b, wb, z, za_col_block, zb_col_block, *, tm, tn, name):
    T, K = a.shape
    N = wa.shape[1]
    blocks = [2 * _nbytes((tm, K), BF16), 2 * _nbytes((K, tn), BF16), 3 * _nbytes((tm, tn), BF16)]
    return pl.pallas_call(
        _gated_pair_kernel,
        grid=(T // tm, N // tn),
        in_specs=[
            pl.BlockSpec((tm, K), lambda i, j: (i, 0)),
            pl.BlockSpec((K, tn), lambda i, j: (0, j)),
            pl.BlockSpec((tm, tn), lambda i, j: (i, za_col_block + j)),
            pl.BlockSpec((tm, K), lambda i, j: (i, 0)),
            pl.BlockSpec((K, tn), lambda i, j: (0, j)),
            pl.BlockSpec((tm, tn), lambda i, j: (i, zb_col_block + j)),
        ],
        out_specs=pl.BlockSpec((tm, tn), lambda i, j: (i, j)),
        out_shape=jax.ShapeDtypeStruct((T, N), BF16),
        compiler_params=_params(("parallel", "parallel"), blocks),
        name=name,
    )(a, wa, z, b, wb, z)


def _residual_matmul_kernel(x_ref, w_ref, r_ref, o_ref):
    o_ref[...] = r_ref[...] + _mm(x_ref[...], w_ref[...])


def residual_matmul(x, w, r, *, tm, tn, name):
    T, K = x.shape
    N = w.shape[1]
    blocks = [_nbytes((tm, K), BF16), _nbytes((K, tn), BF16), 2 * _nbytes((tm, tn), F32)]
    return pl.pallas_call(
        _residual_matmul_kernel,
        grid=(T // tm, N // tn),
        in_specs=[
            pl.BlockSpec((tm, K), lambda i, j: (i, 0)),
            pl.BlockSpec((K, tn), lambda i, j: (0, j)),
            pl.BlockSpec((tm, tn), lambda i, j: (i, j)),
        ],
        out_specs=pl.BlockSpec((tm, tn), lambda i, j: (i, j)),
        out_shape=jax.ShapeDtypeStruct((T, N), F32),
        compiler_params=_params(("parallel", "parallel"), blocks),
        name=name,
    )(x, w, r)


def _rope_prep_kernel(pos_ref, invf_ref, ka_ref, kb_ref, kr_ref, cs_ref, *, scale):
    ang = pos_ref[...].astype(F32) * invf_ref[...]
    c = jnp.cos(ang)
    s = jnp.sin(ang)
    kr_ref[...] = (ka_ref[...] * c + kb_ref[...] * s).astype(kr_ref.dtype)
    lane = lax.broadcasted_iota(jnp.int32, c.shape, 1)
    cs_ref[...] = jnp.where(lane < QK_ROPE, c, s) * scale


def rope_prep(pos, invf, small, ka_col_block, kb_col_block, *, tm, scale):
    T = pos.shape[0]
    blocks = [4 * _nbytes((tm, LANES), F32), _nbytes((tm, LANES), F32)]
    return pl.pallas_call(
        functools.partial(_rope_prep_kernel, scale=scale),
        grid=(T // tm,),
        in_specs=[
            pl.BlockSpec((tm, 1), lambda i: (i, 0)),
            pl.BlockSpec((1, LANES), lambda i: (0, 0)),
            pl.BlockSpec((tm, LANES), lambda i: (i, ka_col_block)),
            pl.BlockSpec((tm, LANES), lambda i: (i, kb_col_block)),
        ],
        out_specs=[pl.BlockSpec((tm, LANES), lambda i: (i, 0)), pl.BlockSpec((tm, LANES), lambda i: (i, 0))],
        out_shape=[jax.ShapeDtypeStruct((T, LANES), BF16), jax.ShapeDtypeStruct((T, LANES), F32)],
        compiler_params=_params(("parallel",), blocks),
        name="rope_prep",
    )(pos, invf, small, small)


def _attention_kernel(q_ref, cs_ref, kn_ref, kr_ref, v_ref, o_ref, kcat_ref, *, scale):
    @pl.when(pl.program_id(2) == 0)
    def _():
        kcat_ref[:, :QK_NOPE] = kn_ref[...]
        kcat_ref[:, QK_NOPE:] = kr_ref[...]

    q = q_ref[...]
    qn = (q[:, :QK_NOPE].astype(F32) * scale).astype(BF16)
    qr = (q[:, QK_NOPE:].astype(F32) * cs_ref[...]).astype(BF16)
    qq = jnp.concatenate([qn, qr], axis=1)
    s = _mm_nt(qq, kcat_ref[...])
    m = jnp.max(s, axis=-1, keepdims=True)
    p = jnp.exp(s - m)
    l = jnp.sum(p, axis=-1, keepdims=True)
    o = _mm(p.astype(BF16), v_ref[...])
    o_ref[...] = (o / l).astype(o_ref.dtype)


def mla_attention(q2, cs, kv, kr2, *, B, S, tq):
    T = B * S
    H = MLA_HEADS
    nq = S // tq
    scale = (QK_NOPE + QK_ROPE) ** -0.5
    blocks = [_nbytes((tq, 2 * LANES), BF16), _nbytes((tq, LANES), F32), 3 * _nbytes((S, LANES), BF16),
              _nbytes((tq, LANES), BF16)]
    scratch = _nbytes((S, 2 * LANES), BF16) + 3 * _nbytes((tq, S), F32)
    return pl.pallas_call(
        functools.partial(_attention_kernel, scale=scale),
        grid=(B, H, nq),
        in_specs=[
            pl.BlockSpec((tq, 2 * LANES), lambda b, h, i: (b * nq + i, h)),
            pl.BlockSpec((tq, LANES), lambda b, h, i: (b * nq + i, 0)),
            pl.BlockSpec((S, LANES), lambda b, h, i: (b, h)),
            pl.BlockSpec((S, LANES), lambda b, h, i: (b, 0)),
            pl.BlockSpec((S, LANES), lambda b, h, i: (b, H + h)),
        ],
        out_specs=pl.BlockSpec((tq, LANES), lambda b, h, i: (b * nq + i, h)),
        out_shape=jax.ShapeDtypeStruct((T, H * V_HEAD), BF16),
        scratch_shapes=[pltpu.VMEM((S, 2 * LANES), BF16)],
        compiler_params=_params(("parallel", "parallel", "arbitrary"), blocks, scratch),
        name="mla_attention",
    )(q2, cs, kv, kr2, kv)


def _conv_kernel(x_ref, w_ref, o_ref, xp_ref, *, S, rows, n_qk_blocks, n_q_blocks):
    pad = 8
    cb = pl.program_id(1)
    xp_ref[0:pad, :] = jnp.zeros((pad, LANES), F32)
    xp_ref[S + pad:S + 2 * pad, :] = jnp.zeros((pad, LANES), F32)
    for t in range(S // rows):
        xp_ref[pad + t * rows:pad + (t + 1) * rows, :] = x_ref[t * rows:(t + 1) * rows, :].astype(F32)
    w = w_ref[...]
    half = CONV_W // 2
    q_scale = jnp.where(cb < n_q_blocks, GDN_DK ** -0.5, 1.0).astype(F32)
    for t in range(S // rows):
        acc = jnp.zeros((rows, LANES), F32)
        for k in range(CONV_W):
            off = pad + t * rows + k - half
            acc = acc + xp_ref[off:off + rows, :] * w[k:k + 1, :]
        y = acc * jax.nn.sigmoid(acc)
        ss = jnp.sum(y * y, axis=-1, keepdims=True)
        f = jnp.where(cb < n_qk_blocks, lax.rsqrt(ss + EPS) * q_scale, 1.0)
        o_ref[t * rows:(t + 1) * rows, :] = (y * f).astype(o_ref.dtype)


def gdn_conv(big, conv_w, *, B, S):
    T = B * S
    C = conv_w.shape[1]
    ncb = C // LANES
    n_q = GDN_HEADS * GDN_DK // LANES
    blocks = [2 * _nbytes((S, LANES), BF16), _nbytes((8, LANES), F32)]
    return pl.pallas_call(
        functools.partial(_conv_kernel, S=S, rows=256, n_qk_blocks=2 * n_q, n_q_blocks=n_q),
        grid=(B, ncb),
        in_specs=[
            pl.BlockSpec((S, LANES), lambda b, c: (b, c)),
            pl.BlockSpec((CONV_W, LANES), lambda b, c: (0, c)),
        ],
        out_specs=pl.BlockSpec((S, LANES), lambda b, c: (b, c)),
        out_shape=jax.ShapeDtypeStruct((T, C), BF16),
        scratch_shapes=[pltpu.VMEM((S + 16, LANES), F32)],
        compiler_params=_params(("parallel", "parallel"), blocks, _nbytes((S + 16, LANES), F32)),
        name="gdn_conv",
    )(big, conv_w)


def _gates_kernel(ab_ref, alog_ref, dtb_ref, o_ref, *, tm):
    H = GDN_HEADS
    x = ab_ref[...]
    z = x + dtb_ref[...]
    softplus = jnp.maximum(z, 0.0) + jnp.log(1.0 + jnp.exp(-jnp.abs(z)))
    g = -jnp.exp(alog_ref[...]) * softplus
    beta = jax.nn.sigmoid(x)
    lane = lax.broadcasted_iota(jnp.int32, (CHUNK, LANES), 1)
    ri = lax.broadcasted_iota(jnp.int32, (CHUNK, CHUNK), 0)
    ci = lax.broadcasted_iota(jnp.int32, (CHUNK, CHUNK), 1)
    tril = (ri >= ci).astype(F32)
    triu = (ri <= ci).astype(F32)
    for t in range(tm // CHUNK):
        sl = slice(t * CHUNK, (t + 1) * CHUNK)
        gc = g[sl]
        pre = jnp.dot(tril, gc, precision=HIGHEST, preferred_element_type=F32)
        suf = jnp.dot(triu, gc, precision=HIGHEST, preferred_element_type=F32)
        cum = jnp.where(lane < H, pre, suf)
        o_ref[sl, :] = jnp.where(lane < 2 * H, cum, beta[sl])


def gdn_gates(small, ab_col_block, alog_l, dtb_l, *, tm):
    T = small.shape[0]
    blocks = [2 * _nbytes((tm, LANES), F32)]
    return pl.pallas_call(
        functools.partial(_gates_kernel, tm=tm),
        grid=(T // tm,),
        in_specs=[
            pl.BlockSpec((tm, LANES), lambda i: (i, ab_col_block)),
            pl.BlockSpec((1, LANES), lambda i: (0, 0)),
            pl.BlockSpec((1, LANES), lambda i: (0, 0)),
        ],
        out_specs=pl.BlockSpec((tm, LANES), lambda i: (i, 0)),
        out_shape=jax.ShapeDtypeStruct((T, LANES), F32),
        compiler_params=_params(("parallel",), blocks),
        name="gdn_gates",
    )(small, alog_l, dtb_l)


def _unit_tri_inverse(a, ri, ci):
    b16 = jnp.right_shift(ri, 4) == jnp.right_shift(ci, 4)
    b32 = jnp.right_shift(ri, 5) == jnp.right_shift(ci, 5)

    def mm(x, y):
        return _mm(x.astype(BF16), y.astype(BF16))

    nd = jnp.where(b16, -a, 0.0)
    p = jnp.where(ri == ci, 1.0, nd)
    n2 = mm(nd, nd)
    p = p + mm(p, n2)
    n4 = mm(n2, n2)
    p = p + mm(p, n4)
    n8 = mm(n4, n4)
    p = p + mm(p, n8)
    l1 = jnp.where(b16, 0.0, jnp.where(b32, a, 0.0))
    p = p - mm(p, mm(l1, p))
    l2 = jnp.where(b32, 0.0, a)
    p = p - mm(p, mm(l2, p))
    return p


def _delta_chunk(q, k, v, gc, gr, bc, state, reverse):
    C = CHUNK
    ri = lax.broadcasted_iota(jnp.int32, (C, C), 0)
    ci = lax.broadcasted_iota(jnp.int32, (C, C), 1)
    if reverse:
        incl, strict, gl = ri <= ci, ri < ci, gr[:, 0:1]
    else:
        incl, strict, gl = ri >= ci, ri > ci, gr[:, C - 1:C]
    gram = _mm_nt(jnp.concatenate([q, k], axis=0), k)
    decay = jnp.exp(jnp.where(incl, gc - gr, -jnp.inf))
    a = jnp.where(strict, gram[C:] * bc * decay, 0.0)
    t = _unit_tri_inverse(a, ri, ci)
    eg = jnp.exp(gc)
    qf, kf, vf = q.astype(F32), k.astype(F32), v.astype(F32)
    rhs = jnp.concatenate([vf * bc, kf * (bc * eg)], axis=1).astype(BF16)
    sol = _mm(t.astype(BF16), rhs)
    u, w = sol[:, :GDN_DV], sol[:, GDN_DV:]
    qk = jnp.where(incl, gram[:C] * decay, 0.0)
    q_dec = qf * eg
    k_end = kf * jnp.exp(gl - gc)
    ws = _mm(jnp.concatenate([w, q_dec], axis=0).astype(BF16), state.astype(BF16))
    v_new = u - ws[:C]
    vb = v_new.astype(BF16)
    o = ws[C:] + _mm(qk.astype(BF16), vb)
    new_state = state * jnp.exp(gl) + _mm_tn(k_end.astype(BF16), vb)
    return o, new_state


def _gdn_kernel(q_ref, k_ref, v_ref, gate_ref, col_ref, row_ref, ng_ref, o_ref, st_ref, of_ref, ob_ref, *, S, hg):
    nch = S // CHUNK
    st_ref[...] = jnp.zeros(st_ref.shape, F32)

    def body(c, carry):
        for d in range(2):
            cc = c if d == 0 else nch - 1 - c
            r0 = pl.multiple_of(cc * CHUNK, CHUNK)
            colv = col_ref[pl.ds(r0, CHUNK), :]
            rowv = row_ref[cc]
            dst = of_ref if d == 0 else ob_ref
            for hl in range(hg):
                hs = slice(hl * LANES, (hl + 1) * LANES)
                gi = d * hg + hl
                bi = (2 + d) * hg + hl
                o, ns = _delta_chunk(
                    q_ref[pl.ds(r0, CHUNK), hs], k_ref[pl.ds(r0, CHUNK), hs], v_ref[pl.ds(r0, CHUNK), hs],
                    colv[:, gi:gi + 1], rowv[gi:gi + 1, :], colv[:, bi:bi + 1], st_ref[gi], reverse=(d == 1))
                st_ref[gi] = ns
                dst[pl.ds(r0, CHUNK), hs] = o
        return carry

    lax.fori_loop(0, nch, body, 0)

    rows = 256
    for t in range(S // rows):
        rs = slice(t * rows, (t + 1) * rows)
        for hl in range(hg):
            hs = slice(hl * LANES, (hl + 1) * LANES)
            o = of_ref[rs, hs] + ob_ref[rs, hs]
            ms = jnp.mean(o * o, axis=-1, keepdims=True)
            gate = gate_ref[rs, hs].astype(F32)
            y = o * lax.rsqrt(ms + EPS) * ng_ref[...]
            o_ref[rs, hs] = (y * (gate * jax.nn.sigmoid(gate))).astype(o_ref.dtype)


def gdn_scan(qkvc, big, gate_col_block0, col, row, norm_g, *, B, S):
    T = B * S
    hg = GDN_HG
    ngroups = GDN_HEADS // hg
    w = hg * LANES
    nch = S // CHUNK
    blocks = [5 * _nbytes((S, w), BF16), _nbytes((S, LANES), F32), _nbytes((nch, 8, LANES), F32)]
    scratch = _nbytes((2 * hg, GDN_DK, GDN_DV), F32) + 2 * _nbytes((S, w), F32)
    return pl.pallas_call(
        functools.partial(_gdn_kernel, S=S, hg=hg),
        grid=(B, ngroups),
        in_specs=[
            pl.BlockSpec((S, w), lambda b, g: (b, g)),
            pl.BlockSpec((S, w), lambda b, g: (b, ngroups + g)),
            pl.BlockSpec((S, w), lambda b, g: (b, 2 * ngroups + g)),
            pl.BlockSpec((S, w), lambda b, g: (b, gate_col_block0 + g)),
            pl.BlockSpec((None, None, S, 4 * hg), lambda b, g: (b, g, 0, 0)),
            pl.BlockSpec((None, None, nch, 2 * hg, CHUNK), lambda b, g: (b, g, 0, 0, 0)),
            pl.BlockSpec((1, GDN_DV), lambda b, g: (0, 0)),
        ],
        out_specs=pl.BlockSpec((S, w), lambda b, g: (b, g)),
        out_shape=jax.ShapeDtypeStruct((T, GDN_HEADS * GDN_DV), BF16),
        scratch_shapes=[
            pltpu.VMEM((2 * hg, GDN_DK, GDN_DV), F32),
            pltpu.VMEM((S, w), F32),
            pltpu.VMEM((S, w), F32),
        ],
        compiler_params=_params(("parallel", "parallel"), blocks, scratch),
        name="gdn_scan",
    )(qkvc, qkvc, qkvc, big, col, row, norm_g)


def _router_kernel(h_ref, g_ref, rw_ref, rb_ref, n_ref, idx_ref, gate_ref):
    x = h_ref[...]
    ms = jnp.mean(x * x, axis=-1, keepdims=True)
    n = x * lax.rsqrt(ms + EPS) * g_ref[...]
    n_ref[...] = n
    logits = jnp.dot(n, rw_ref[...], precision=HIGHEST, preferred_element_type=F32) + rb_ref[...]
    lane = lax.broadcasted_iota(jnp.int32, logits.shape, 1).astype(F32)
    neg = -jnp.inf
    l = jnp.where(lane < N_EXPERTS, logits, neg)
    vals, idxs = [], []
    for _ in range(TOP_K):
        m = jnp.max(l, axis=-1, keepdims=True)
        idx = jnp.min(jnp.where(l == m, lane, float(LANES)), axis=-1, keepdims=True)
        vals.append(m)
        idxs.append(idx)
        l = jnp.where(lane == idx, neg, l)
    es = [jnp.exp(v - vals[0]) for v in vals]
    tot = es[0] + es[1] + es[2] + es[3]
    iout = jnp.zeros(logits.shape, F32)
    gout = jnp.zeros(logits.shape, F32)
    for k in range(TOP_K):
        iout = jnp.where(lane == k, idxs[k], iout)
        gout = jnp.where(lane == k, es[k] / tot, gout)
    idx_ref[...] = iout.astype(jnp.int32)
    gate_ref[...] = gout


def moe_router(h, g, rw, rb, *, tm):
    T, D = h.shape
    blocks = [2 * _nbytes((tm, D), F32), _nbytes((D, LANES), F32), 2 * _nbytes((tm, LANES), F32)]
    return pl.pallas_call(
        _router_kernel,
        grid=(T // tm,),
        in_specs=[
            pl.BlockSpec((tm, D), lambda i: (i, 0)),
            pl.BlockSpec((1, D), lambda i: (0, 0)),
            pl.BlockSpec((D, LANES), lambda i: (0, 0)),
            pl.BlockSpec((1, LANES), lambda i: (0, 0)),
        ],
        out_specs=[
            pl.BlockSpec((tm, D), lambda i: (i, 0)),
            pl.BlockSpec((tm, LANES), lambda i: (i, 0)),
            pl.BlockSpec((tm, LANES), lambda i: (i, 0)),
        ],
        out_shape=[
            jax.ShapeDtypeStruct((T, D), F32),
            jax.ShapeDtypeStruct((T, LANES), jnp.int32),
            jax.ShapeDtypeStruct((T, LANES), F32),
        ],
        compiler_params=_params(("parallel",), blocks, 4 * _nbytes((tm, D), F32)),
        name="moe_router",
    )(h, g, rw, rb)


def _row_copy(src_hbm, row, dst_ref, dst_row, sem):
    return pltpu.make_async_copy(src_hbm.at[pl.ds(row, 1), :], dst_ref.at[pl.ds(dst_row, 1), :], sem)


def _dispatch_kernel(tok_ref, nu_ref, src_hbm, o_ref, buf_ref, sem, *, bm):
    i = pl.program_id(0)

    @pl.when(i < nu_ref[0])
    def _():
        def issue(r, c):
            _row_copy(src_hbm, tok_ref[i * bm + r], buf_ref, r, sem).start()
            return c

        lax.fori_loop(0, bm, issue, 0)
        pltpu.make_async_copy(src_hbm.at[pl.ds(0, bm), :], buf_ref, sem).wait()
        o_ref[...] = buf_ref[...].astype(o_ref.dtype)

    @pl.when(i >= nu_ref[0])
    def _():
        o_ref[...] = jnp.zeros(o_ref.shape, o_ref.dtype)


def moe_dispatch(row_tok, n_used, src, *, rows, bm):
    D = src.shape[1]
    blocks = [_nbytes((bm, D), BF16)]
    return pl.pallas_call(
        functools.partial(_dispatch_kernel, bm=bm),
        grid_spec=pltpu.PrefetchScalarGridSpec(
            num_scalar_prefetch=2,
            grid=(rows // bm,),
            in_specs=[pl.BlockSpec(memory_space=pl.ANY)],
            out_specs=pl.BlockSpec((bm, D), lambda i, tok, nu: (i, 0)),
            scratch_shapes=[pltpu.VMEM((bm, D), F32), pltpu.SemaphoreType.DMA],
        ),
        out_shape=jax.ShapeDtypeStruct((rows, D), BF16),
        compiler_params=_params(("arbitrary",), blocks, _nbytes((bm, D), F32)),
        name="moe_dispatch",
    )(row_tok, n_used, src)


def _expert_up_kernel(be_ref, nu_ref, x_ref, wg_ref, wu_ref, bg_ref, bu_ref, o_ref):
    i = pl.program_id(1)

    @pl.when(i < nu_ref[0])
    def _():
        x = x_ref[...]
        gate = _mm(x, wg_ref[...]) + bg_ref[...]
        up = _mm(x, wu_ref[...]) + bu_ref[...]
        gate = jnp.minimum(gate, SWIGLU_LIMIT)
        up = jnp.clip(up, -SWIGLU_LIMIT, SWIGLU_LIMIT)
        act = (up + 1.0) * gate * jax.nn.sigmoid(SWIGLU_ALPHA * gate)
        o_ref[...] = act.astype(o_ref.dtype)

    @pl.when(i >= nu_ref[0])
    def _():
        o_ref[...] = jnp.zeros(o_ref.shape, o_ref.dtype)


def moe_expert_up(block_expert, n_used, xs, w_gu, b_gu, *, bm, tn):
    R, D = xs.shape
    dff = w_gu.shape[2] // 2
    nj = dff // tn
    nb = R // bm

    def last(i, nu):
        return jnp.minimum(i, nu[0] - 1)

    blocks = [_nbytes((bm, D), BF16), 2 * _nbytes((D, tn), BF16), 2 * _nbytes((8, tn), F32), _nbytes((bm, tn), BF16)]
    return pl.pallas_call(
        _expert_up_kernel,
        grid_spec=pltpu.PrefetchScalarGridSpec(
            num_scalar_prefetch=2,
            grid=(nj, nb),
            in_specs=[
                pl.BlockSpec((bm, D), lambda j, i, be, nu: (last(i, nu), 0)),
                pl.BlockSpec((None, D, tn), lambda j, i, be, nu: (be[last(i, nu)], 0, j)),
                pl.BlockSpec((None, D, tn), lambda j, i, be, nu: (be[last(i, nu)], 0, nj + j)),
                pl.BlockSpec((None, 1, tn), lambda j, i, be, nu: (be[last(i, nu)], 0, j)),
                pl.BlockSpec((None, 1, tn), lambda j, i, be, nu: (be[last(i, nu)], 0, nj + j)),
            ],
            out_specs=pl.BlockSpec((bm, tn), lambda j, i, be, nu: (i, j)),
        ),
        out_shape=jax.ShapeDtypeStruct((R, dff), BF16),
        compiler_params=_params(("arbitrary", "arbitrary"), blocks, 4 * _nbytes((bm, tn), F32)),
        name="moe_expert_up",
    )(block_expert, n_used, xs, w_gu, w_gu, b_gu, b_gu)


def _expert_down_kernel(be_ref, nu_ref, a_ref, w_ref, b_ref, g_ref, o_ref):
    i = pl.program_id(1)

    @pl.when(i < nu_ref[0])
    def _():
        o_ref[...] = (_mm(a_ref[...], w_ref[...]) + b_ref[...]) * g_ref[...]

    @pl.when(i >= nu_ref[0])
    def _():
        o_ref[...] = jnp.zeros(o_ref.shape, o_ref.dtype)


def moe_expert_down(block_expert, n_used, act, w_down, b_down, row_gate, *, bm, tn):
    R, dff = act.shape
    D = w_down.shape[2]

    def last(i, nu):
        return jnp.minimum(i, nu[0] - 1)

    blocks = [_nbytes((bm, dff), BF16), _nbytes((dff, tn), BF16), _nbytes((8, tn), F32), _nbytes((bm, LANES), F32),
              _nbytes((bm, tn), F32)]
    return pl.pallas_call(
        _expert_down_kernel,
        grid_spec=pltpu.PrefetchScalarGridSpec(
            num_scalar_prefetch=2,
            grid=(D // tn, R // bm),
            in_specs=[
                pl.BlockSpec((bm, dff), lambda j, i, be, nu: (last(i, nu), 0)),
                pl.BlockSpec((None, dff, tn), lambda j, i, be, nu: (be[last(i, nu)], 0, j)),
                pl.BlockSpec((None, 1, tn), lambda j, i, be, nu: (be[last(i, nu)], 0, j)),
                pl.BlockSpec((bm, 1), lambda j, i, be, nu: (last(i, nu), 0)),
            ],
            out_specs=pl.BlockSpec((bm, tn), lambda j, i, be, nu: (i, j)),
        ),
        out_shape=jax.ShapeDtypeStruct((R, D), F32),
        compiler_params=_params(("arbitrary", "arbitrary"), blocks, _nbytes((bm, tn), F32)),
        name="moe_expert_down",
    )(block_expert, n_used, act, w_down, b_down, row_gate)


def _combine_kernel(dest_ref, h_ref, y_hbm, g_ref, o_ref, buf_ref, sem, *, tb):
    i = pl.program_id(0)

    def issue(r, c):
        for k in range(TOP_K):
            _row_copy(y_hbm, dest_ref[(i * tb + r) * TOP_K + k], buf_ref.at[k], r, sem).start()
        return c

    lax.fori_loop(0, tb, issue, 0)
    for k in range(TOP_K):
        pltpu.make_async_copy(y_hbm.at[pl.ds(0, tb), :], buf_ref.at[k], sem).wait()

    def body(r, c):
        sl = pl.ds(pl.multiple_of(r * NORM_ROWS, NORM_ROWS), NORM_ROWS)
        hv = h_ref[sl, :]
        for k in range(TOP_K):
            hv = hv + buf_ref[k, sl, :]
        ms = jnp.mean(hv * hv, axis=-1, keepdims=True)
        o_ref[sl, :] = hv * lax.rsqrt(ms + EPS) * g_ref[...]
        return c

    lax.fori_loop(0, tb // NORM_ROWS, body, 0)


def moe_combine(dest, h, y, g, *, tb):
    T, D = h.shape
    blocks = [2 * _nbytes((tb, D), F32)]
    return pl.pallas_call(
        functools.partial(_combine_kernel, tb=tb),
        grid_spec=pltpu.PrefetchScalarGridSpec(
            num_scalar_prefetch=1,
            grid=(T // tb,),
            in_specs=[
                pl.BlockSpec((tb, D), lambda i, d: (i, 0)),
                pl.BlockSpec(memory_space=pl.ANY),
                pl.BlockSpec((1, D), lambda i, d: (0, 0)),
            ],
            out_specs=pl.BlockSpec((tb, D), lambda i, d: (i, 0)),
            scratch_shapes=[pltpu.VMEM((TOP_K, tb, D), F32), pltpu.SemaphoreType.DMA],
        ),
        out_shape=jax.ShapeDtypeStruct((T, D), F32),
        compiler_params=_params(("arbitrary",), blocks, _nbytes((TOP_K, tb, D), F32)),
        name="moe_combine",
    )(dest, h, y, g)


def _routing_tables(top_idx, top_gate, *, bm):
    T = top_idx.shape[0]
    tk = T * TOP_K
    flat_e = top_idx.reshape(tk)
    onehot = (flat_e[:, None] == jnp.arange(N_EXPERTS, dtype=jnp.int32)[None, :]).astype(jnp.int32)
    csum = jnp.cumsum(onehot, axis=0)
    rank = jnp.take_along_axis(csum, flat_e[:, None], axis=1)[:, 0] - 1
    counts = csum[-1]
    padded = (counts + bm - 1) // bm * bm
    pad_end = jnp.cumsum(padded)
    pad_start = pad_end - padded
    dest = (pad_start[flat_e] + rank).astype(jnp.int32)
    nb = tk // bm + N_EXPERTS
    rows = nb * bm
    flat_tok = jnp.repeat(jnp.arange(T, dtype=jnp.int32), TOP_K)
    row_tok = jnp.zeros((rows,), jnp.int32).at[dest].set(flat_tok)
    row_gate = jnp.zeros((rows,), F32).at[dest].set(top_gate.reshape(tk))
    block_expert = jnp.minimum(
        jnp.searchsorted(pad_end, jnp.arange(nb, dtype=pad_end.dtype) * bm, side="right"), N_EXPERTS - 1
    ).astype(jnp.int32)
    n_used = (pad_end[-1] // bm).astype(jnp.int32).reshape(1)
    return dest, row_tok, row_gate.reshape(rows, 1), block_expert, n_used, rows


def _rot_half(w):
    half = QK_ROPE // 2
    return jnp.concatenate([-w[..., half:], w[..., :half]], axis=-1)


def _layer(h, pos, invf, ln1_g, w_in, q_norm_g, w_q_up, kv_norm_g, w_kv_up, conv_w, a_log, dt_bias, gdn_norm_g,
           w_mla_o, w_gdn_o, w_out, ln2_g, router_w, router_b, w_gu, b_gu, w_down, b_down, out_g, *, B, S):
    T, D = h.shape
    H = MLA_HEADS
    gqkv = 2 * GDN_HEADS * GDN_DK + GDN_HEADS * GDN_DV
    gw = GDN_HEADS * GDN_DV
    o0 = Q_LORA + KV_LORA
    c_kr, c_qkv = o0, o0 + QK_ROPE
    c_gate = c_qkv + gqkv
    c_ab = c_gate + gw
    c_za = c_ab + 4 * GDN_HEADS
    c_zb = c_za + D

    w_kr = w_in[:, c_kr:c_qkv]
    w_kt = _rot_half(w_kr)
    small_cols = [w_in[:, :o0], w_kr, w_kr, w_kt, w_kt, w_in[:, c_ab:c_za]]
    n_small = 2048
    used = o0 + 4 * QK_ROPE + 4 * GDN_HEADS
    small_cols.append(jnp.zeros((D, n_small - used), F32))
    w_small = jnp.concatenate(small_cols, axis=1).astype(BF16)
    w_big = jnp.concatenate([w_in[:, c_qkv:c_gate], w_in[:, c_gate:c_ab], w_in[:, c_za:c_zb], w_in[:, c_zb:]],
                            axis=1).astype(BF16)
    ka_blk = o0 // LANES
    kb_blk = ka_blk + 1
    ab_blk = ka_blk + 2

    wq = w_q_up.reshape(Q_LORA, H, QK_NOPE + QK_ROPE)
    wq_rope = wq[:, :, QK_NOPE:]
    wq2 = jnp.concatenate([wq[:, :, :QK_NOPE], wq_rope, _rot_half(wq_rope)], axis=-1).reshape(Q_LORA, H * 2 * LANES)
    wq2 = wq2.astype(BF16)
    wkv = w_kv_up.reshape(KV_LORA, H, QK_NOPE + V_HEAD)
    wkv2 = jnp.concatenate([wkv[:, :, :QK_NOPE].reshape(KV_LORA, H * QK_NOPE),
                            wkv[:, :, QK_NOPE:].reshape(KV_LORA, H * V_HEAD)], axis=1).astype(BF16)

    small = norm_matmul(h, ln1_g, w_small, out_dtype=F32, tm=512, tn=1024, name="in_proj_small")
    big = norm_matmul(h, ln1_g, w_big, out_dtype=BF16, tm=512, tn=1024, name="in_proj_big")

    q2 = norm_matmul(small, q_norm_g, wq2, out_dtype=BF16, tm=1024, tn=1024, x_col_block=0, name="q_up")
    kv = norm_matmul(small, kv_norm_g, wkv2, out_dtype=BF16, tm=1024, tn=1024, x_col_block=Q_LORA // KV_LORA,
                     name="kv_up")
    kr2, cs = rope_prep(pos, invf, small, ka_blk, kb_blk, tm=1024, scale=(QK_NOPE + QK_ROPE) ** -0.5)
    o_a = mla_attention(q2, cs, kv, kr2, B=B, S=S, tq=256)

    qkvc = gdn_conv(big, conv_w, B=B, S=S)
    pad = jnp.zeros((LANES - 2 * GDN_HEADS,), F32)
    alog_l = jnp.concatenate([a_log.reshape(-1), pad]).reshape(1, LANES)
    dtb_l = jnp.concatenate([dt_bias.reshape(-1), pad]).reshape(1, LANES)
    gp = gdn_gates(small, ab_blk, alog_l, dtb_l, tm=512)
    hg = GDN_HG
    ng = GDN_HEADS // hg
    nch = S // CHUNK
    gp4 = gp[:, :4 * GDN_HEADS].reshape(B, S, 4, ng, hg)
    col = gp4.transpose(0, 3, 1, 2, 4).reshape(B, ng, S, 4 * hg)
    row = gp4[:, :, :2].reshape(B, nch, CHUNK, 2, ng, hg).transpose(0, 4, 1, 3, 5, 2).reshape(B, ng, nch, 2 * hg, CHUNK)
    o_b = gdn_scan(qkvc, big, gqkv // (hg * LANES), col, row, gdn_norm_g.reshape(1, GDN_DV), B=B, S=S)

    merged = gated_pair_matmul(o_a, w_mla_o.astype(BF16), o_b, w_gdn_o.astype(BF16), big,
                               (gqkv + gw) // 1024, (gqkv + gw + D) // 1024, tm=512, tn=1024, name="mixer_out")
    h1 = residual_matmul(merged, w_out.astype(BF16), h, tm=512, tn=1024, name="out_proj")

    rw = jnp.concatenate([router_w, jnp.zeros((D, LANES - N_EXPERTS), F32)], axis=1)
    rb = jnp.concatenate([router_b, jnp.zeros((LANES - N_EXPERTS,), F32)]).reshape(1, LANES)
    n2, top_idx, top_gate = moe_router(h1, ln2_g, rw, rb, tm=256)
    dest, row_tok, row_gate, block_expert, n_used, rows = _routing_tables(top_idx[:, :TOP_K], top_gate[:, :TOP_K],
                                                                          bm=MOE_BM)
    xs = moe_dispatch(row_tok, n_used, n2, rows=rows, bm=MOE_BM)
    dff = w_gu.shape[2] // 2
    act = moe_expert_up(block_expert, n_used, xs, w_gu.astype(BF16), b_gu.reshape(N_EXPERTS, 1, 2 * dff),
                        bm=MOE_BM, tn=512)
    y = moe_expert_down(block_expert, n_used, act, w_down.astype(BF16), b_down.reshape(N_EXPERTS, 1, D), row_gate,
                        bm=MOE_BM, tn=1024)
    return moe_combine(dest, h1, y, out_g, tb=128)


def kernel(x, positions, ln1_g, w_in, q_norm_g, w_q_up, kv_norm_g, w_kv_up, gdn_conv_w, gdn_a_log, gdn_dt_bias,
           gdn_norm_g, w_mla_o, w_gdn_o, w_out, ln2_g, router_w, router_b, w_gu, b_gu, w_down, b_down, ln_f_g):
    B, S, D = x.shape
    depth = ln1_g.shape[0]
    assert depth == 1, "the final norm is fused into the last layer's expert combine"
    half = QK_ROPE // 2
    inv_freq = ROPE_THETA ** (-jnp.arange(0, QK_ROPE, 2, dtype=F32) / QK_ROPE)
    invf = jnp.tile(inv_freq, LANES // half).reshape(1, LANES)
    pos = positions.reshape(B * S, 1).astype(jnp.int32)
    h = x.reshape(B * S, D)
    l = 0
    out = _layer(h, pos, invf, ln1_g[l].reshape(1, D), w_in[l], q_norm_g[l].reshape(1, Q_LORA), w_q_up[l],
                 kv_norm_g[l].reshape(1, KV_LORA), w_kv_up[l], gdn_conv_w[l], gdn_a_log[l], gdn_dt_bias[l],
                 gdn_norm_g[l], w_mla_o[l], w_gdn_o[l], w_out[l], ln2_g[l].reshape(1, D), router_w[l], router_b[l],
                 w_gu[l], b_gu[l], w_down[l], b_down[l], ln_f_g.reshape(1, D), B=B, S=S)
    return out.reshape(B, S, D)
```

```python
import functools

import jax
import jax.numpy as jnp
from jax import lax
from jax.experimental import pallas as pl
from jax.experimental.pallas import tpu as pltpu

F32 = jnp.float32
BF16 = jnp.bfloat16

EPS = 1e-6
MLA_HEADS = 16
Q_LORA = 1024
KV_LORA = 512
QK_NOPE = 128
QK_ROPE = 64
V_HEAD = 128
ROPE_THETA = 10000.0
GDN_HEADS = 16
GDN_DK = 128
GDN_DV = 128
CONV_W = 5
CHUNK = 64
N_EXPERTS = 32
TOP_K = 4
SWIGLU_LIMIT = 7.0
SWIGLU_ALPHA = 1.702

LANES = 128
V7X_VMEM_BYTES = 64 * 1024 * 1024
VMEM_CAP = V7X_VMEM_BYTES - 8 * 1024 * 1024
LOG2_E = 1.4426950408889634
ATTN_HEADS_PER_STEP = 2
NORM_ROWS = 32
GDN_HG = 2
GDN_WIN = 256
MOE_BM = 256
HIGHEST = lax.Precision.HIGHEST


def _nbytes(shape, dtype):
    n = 1
    for s in shape:
        n *= s
    return n * jnp.dtype(dtype).itemsize


def _params(sem, blocks, scratch=0):
    need = 2 * sum(blocks) + scratch + 16 * 1024 * 1024
    return pltpu.CompilerParams(dimension_semantics=sem, vmem_limit_bytes=int(min(need, VMEM_CAP)))


def _mm(a, b):
    return jnp.dot(a, b, preferred_element_type=F32)


def _mm_nt(a, b):
    return lax.dot_general(a, b, (((1,), (1,)), ((), ())), preferred_element_type=F32)


def _mm_tn(a, b):
    return lax.dot_general(a, b, (((0,), (0,)), ((), ())), preferred_element_type=F32)


def _norm_rows_into(x_ref, g_ref, xn_ref, rows):
    def body(r, c):
        sl = pl.ds(pl.multiple_of(r * NORM_ROWS, NORM_ROWS), NORM_ROWS)
        xv = x_ref[sl, :].astype(F32)
        ms = jnp.mean(xv * xv, axis=-1, keepdims=True)
        xn_ref[sl, :] = (xv * lax.rsqrt(ms + EPS) * g_ref[...]).astype(BF16)
        return c

    lax.fori_loop(0, rows // NORM_ROWS, body, 0)


def _norm_matmul_kernel(x_ref, g_ref, w_ref, o_ref, xn_ref, *, tm):
    @pl.when(pl.program_id(1) == 0)
    def _():
        _norm_rows_into(x_ref, g_ref, xn_ref, tm)

    o_ref[...] = _mm(xn_ref[...], w_ref[...]).astype(o_ref.dtype)


def norm_matmul(x, g, w, *, out_dtype, tm, tn, x_col_block=0, name):
    T = x.shape[0]
    K, N = w.shape
    tm = min(tm, T)
    assert T % tm == 0 and N % tn == 0 and tm % NORM_ROWS == 0
    blocks = [_nbytes((tm, K), x.dtype), _nbytes((K, tn), BF16), _nbytes((tm, tn), out_dtype)]
    return pl.pallas_call(
        functools.partial(_norm_matmul_kernel, tm=tm),
        grid=(T // tm, N // tn),
        in_specs=[
            pl.BlockSpec((tm, K), lambda i, j: (i, x_col_block)),
            pl.BlockSpec((1, K), lambda i, j: (0, 0)),
            pl.BlockSpec((K, tn), lambda i, j: (0, j)),
        ],
        out_specs=pl.BlockSpec((tm, tn), lambda i, j: (i, j)),
        out_shape=jax.ShapeDtypeStruct((T, N), out_dtype),
        scratch_shapes=[pltpu.VMEM((tm, K), BF16)],
        compiler_params=_params(("parallel", "arbitrary"), blocks, _nbytes((tm, K), BF16)),
        name=name,
    )(x, g, w)


def _gated_pair_kernel(a_ref, wa_ref, za_ref, b_ref, wb_ref, zb_ref, o_ref):
    ya = _mm(a_ref[...], wa_ref[...])
    yb = _mm(b_ref[...], wb_ref[...])
    ga = jax.nn.sigmoid(za_ref[...].astype(F32))
    gb = jax.nn.sigmoid(zb_ref[...].astype(F32))
    o_ref[...] = (ga * ya + gb * yb).astype(o_ref.dtype)


def gated_pair_matmul(a, wa, b, wb, z, za_col_block, zb_col_block, *, tm, tn, name):
    T, K = a.shape
    N = wa.shape[1]
    blocks = [2 * _nbytes((tm, K), BF16), 2 * _nbytes((K, tn), BF16), 3 * _nbytes((tm, tn), BF16)]
    return pl.pallas_call(
        _gated_pair_kernel,
        grid=(T // tm, N // tn),
        in_specs=[
            pl.BlockSpec((tm, K), lambda i, j: (i, 0)),
            pl.BlockSpec((K, tn), lambda i, j: (0, j)),
            pl.BlockSpec((tm, tn), lambda i, j: (i, za_col_block + j)),
            pl.BlockSpec((tm, K), lambda i, j: (i, 0)),
            pl.BlockSpec((K, tn), lambda i, j: (0, j)),
            pl.BlockSpec((tm, tn), lambda i, j: (i, zb_col_block + j)),
        ],
        out_specs=pl.BlockSpec((tm, tn), lambda i, j: (i, j)),
        out_shape=jax.ShapeDtypeStruct((T, N), BF16),
        compiler_params=_params(("parallel", "parallel"), blocks),
        name=name,
    )(a, wa, z, b, wb, z)


def _residual_matmul_kernel(x_ref, w_ref, r_ref, o_ref):
    o_ref[...] = r_ref[...] + _mm(x_ref[...], w_ref[...])


def residual_matmul(x, w, r, *, tm, tn, name):
    T, K = x.shape
    N = w.shape[1]
    blocks = [_nbytes((tm, K), BF16), _nbytes((K, tn), BF16), 2 * _nbytes((tm, tn), F32)]
    return pl.pallas_call(
        _residual_matmul_kernel,
        grid=(T // tm, N // tn),
        in_specs=[
            pl.BlockSpec((tm, K), lambda i, j: (i, 0)),
            pl.BlockSpec((K, tn), lambda i, j: (0, j)),
            pl.BlockSpec((tm, tn), lambda i, j: (i, j)),
        ],
        out_specs=pl.BlockSpec((tm, tn), lambda i, j: (i, j)),
        out_shape=jax.ShapeDtypeStruct((T, N), F32),
        compiler_params=_params(("parallel", "parallel"), blocks),
        name=name,
    )(x, w, r)


def _rope_prep_kernel(pos_ref, invf_ref, ka_ref, kb_ref, kr_ref, cs_ref, *, scale):
    ang = pos_ref[...].astype(F32) * invf_ref[...]
    c = jnp.cos(ang)
    s = jnp.sin(ang)
    kr_ref[...] = (ka_ref[...] * c + kb_ref[...] * s).astype(kr_ref.dtype)
    lane = lax.broadcasted_iota(jnp.int32, c.shape, 1)
    cs_ref[...] = jnp.where(lane < QK_ROPE, c, s) * scale


def rope_prep(pos, invf, small, ka_col_block, kb_col_block, *, tm, scale):
    T = pos.shape[0]
    blocks = [4 * _nbytes((tm, LANES), F32), _nbytes((tm, LANES), F32)]
    return pl.pallas_call(
        functools.partial(_rope_prep_kernel, scale=scale),
        grid=(T // tm,),
        in_specs=[
            pl.BlockSpec((tm, 1), lambda i: (i, 0)),
            pl.BlockSpec((1, LANES), lambda i: (0, 0)),
            pl.BlockSpec((tm, LANES), lambda i: (i, ka_col_block)),
            pl.BlockSpec((tm, LANES), lambda i: (i, kb_col_block)),
        ],
        out_specs=[pl.BlockSpec((tm, LANES), lambda i: (i, 0)), pl.BlockSpec((tm, LANES), lambda i: (i, 0))],
        out_shape=[jax.ShapeDtypeStruct((T, LANES), BF16), jax.ShapeDtypeStruct((T, LANES), F32)],
        compiler_params=_params(("parallel",), blocks),
        name="rope_prep",
    )(pos, invf, small, small)


def _attention_kernel(q_ref, cs_ref, kn_ref, kr_ref, v_ref, o_ref, kcat_ref, *, scale, nh):
    @pl.when(pl.program_id(2) == 0)
    def _():
        for a in range(nh):
            kcat_ref[a, :, :QK_NOPE] = kn_ref[:, a * QK_NOPE:(a + 1) * QK_NOPE]
            kcat_ref[a, :, QK_NOPE:] = kr_ref[...]

    heads = range(nh)
    qq = []
    for a in heads:
        q = q_ref[:, a * 2 * LANES:(a + 1) * 2 * LANES]
        qn = (q[:, :QK_NOPE].astype(F32) * scale).astype(BF16)
        qr = (q[:, QK_NOPE:].astype(F32) * cs_ref[...]).astype(BF16)
        qq.append(jnp.concatenate([qn, qr], axis=1))
    s = [_mm_nt(qq[a], kcat_ref[a]) for a in heads]
    m = [jnp.max(s[a], axis=-1, keepdims=True) for a in heads]
    p = [jnp.exp2(s[a] - m[a]) for a in heads]
    l = [jnp.sum(p[a], axis=-1, keepdims=True) for a in heads]
    o = [_mm(p[a].astype(BF16), v_ref[:, a * V_HEAD:(a + 1) * V_HEAD]) for a in heads]
    for a in heads:
        o_ref[:, a * V_HEAD:(a + 1) * V_HEAD] = (o[a] / l[a]).astype(o_ref.dtype)


def mla_attention(q2, cs, kv, kr2, *, B, S, tq, scale):
    T = B * S
    H = MLA_HEADS
    nh = ATTN_HEADS_PER_STEP
    nq = S // tq
    wq, wk = nh * 2 * LANES, nh * LANES
    blocks = [_nbytes((tq, wq), BF16), _nbytes((tq, LANES), F32), 2 * _nbytes((S, wk), BF16),
              _nbytes((S, LANES), BF16), _nbytes((tq, wk), BF16)]
    scratch = _nbytes((nh, S, 2 * LANES), BF16) + 3 * nh * _nbytes((tq, S), F32)
    return pl.pallas_call(
        functools.partial(_attention_kernel, scale=scale, nh=nh),
        grid=(B, H // nh, nq),
        in_specs=[
            pl.BlockSpec((tq, wq), lambda b, h, i: (b * nq + i, h)),
            pl.BlockSpec((tq, LANES), lambda b, h, i: (b * nq + i, 0)),
            pl.BlockSpec((S, wk), lambda b, h, i: (b, h)),
            pl.BlockSpec((S, LANES), lambda b, h, i: (b, 0)),
            pl.BlockSpec((S, wk), lambda b, h, i: (b, H // nh + h)),
        ],
        out_specs=pl.BlockSpec((tq, wk), lambda b, h, i: (b * nq + i, h)),
        out_shape=jax.ShapeDtypeStruct((T, H * V_HEAD), BF16),
        scratch_shapes=[pltpu.VMEM((nh, S, 2 * LANES), BF16)],
        compiler_params=_params(("parallel", "parallel", "arbitrary"), blocks, scratch),
        name="mla_attention",
    )(q2, cs, kv, kr2, kv)


def _conv_kernel(x_ref, w_ref, o_ref, xp_ref, *, S, rows, n_qk_blocks, n_q_blocks):
    pad = 8
    cb = pl.program_id(1)
    xp_ref[0:pad, :] = jnp.zeros((pad, LANES), F32)
    xp_ref[S + pad:S + 2 * pad, :] = jnp.zeros((pad, LANES), F32)
    for t in range(S // rows):
        xp_ref[pad + t * rows:pad + (t + 1) * rows, :] = x_ref[t * rows:(t + 1) * rows, :].astype(F32)
    w = w_ref[...]
    half = CONV_W // 2
    q_scale = jnp.where(cb < n_q_blocks, GDN_DK ** -0.5, 1.0).astype(F32)
    for t in range(S // rows):
        acc = jnp.zeros((rows, LANES), F32)
        for k in range(CONV_W):
            off = pad + t * rows + k - half
            acc = acc + xp_ref[off:off + rows, :] * w[k:k + 1, :]
        y = acc * jax.nn.sigmoid(acc)
        ss = jnp.sum(y * y, axis=-1, keepdims=True)
        f = jnp.where(cb < n_qk_blocks, lax.rsqrt(ss + EPS) * q_scale, 1.0)
        o_ref[t * rows:(t + 1) * rows, :] = (y * f).astype(o_ref.dtype)


def gdn_conv(big, conv_w, *, B, S):
    T = B * S
    C = conv_w.shape[1]
    ncb = C // LANES
    n_q = GDN_HEADS * GDN_DK // LANES
    blocks = [2 * _nbytes((S, LANES), BF16), _nbytes((8, LANES), F32)]
    return pl.pallas_call(
        functools.partial(_conv_kernel, S=S, rows=256, n_qk_blocks=2 * n_q, n_q_blocks=n_q),
        grid=(B, ncb),
        in_specs=[
            pl.BlockSpec((S, LANES), lambda b, c: (b, c)),
            pl.BlockSpec((CONV_W, LANES), lambda b, c: (0, c)),
        ],
        out_specs=pl.BlockSpec((S, LANES), lambda b, c: (b, c)),
        out_shape=jax.ShapeDtypeStruct((T, C), BF16),
        scratch_shapes=[pltpu.VMEM((S + 16, LANES), F32)],
        compiler_params=_params(("parallel", "parallel"), blocks, _nbytes((S + 16, LANES), F32)),
        name="gdn_conv",
    )(big, conv_w)


def _gates_kernel(ab_ref, alog_ref, dtb_ref, o_ref, *, tm):
    H = GDN_HEADS
    x = ab_ref[...]
    z = x + dtb_ref[...]
    softplus = jnp.maximum(z, 0.0) + jnp.log(1.0 + jnp.exp(-jnp.abs(z)))
    g = -jnp.exp(alog_ref[...]) * softplus
    beta = jax.nn.sigmoid(x)
    lane = lax.broadcasted_iota(jnp.int32, (CHUNK, LANES), 1)
    ri = lax.broadcasted_iota(jnp.int32, (CHUNK, CHUNK), 0)
    ci = lax.broadcasted_iota(jnp.int32, (CHUNK, CHUNK), 1)
    tril = (ri >= ci).astype(F32)
    triu = (ri <= ci).astype(F32)
    for t in range(tm // CHUNK):
        sl = slice(t * CHUNK, (t + 1) * CHUNK)
        gc = g[sl]
        pre = jnp.dot(tril, gc, precision=HIGHEST, preferred_element_type=F32)
        suf = jnp.dot(triu, gc, precision=HIGHEST, preferred_element_type=F32)
        cum = jnp.where(lane < H, pre, suf)
        o_ref[sl, :] = jnp.where(lane < 2 * H, cum, beta[sl])


def gdn_gates(small, ab_col_block, alog_l, dtb_l, *, tm):
    T = small.shape[0]
    blocks = [2 * _nbytes((tm, LANES), F32)]
    return pl.pallas_call(
        functools.partial(_gates_kernel, tm=tm),
        grid=(T // tm,),
        in_specs=[
            pl.BlockSpec((tm, LANES), lambda i: (i, ab_col_block)),
            pl.BlockSpec((1, LANES), lambda i: (0, 0)),
            pl.BlockSpec((1, LANES), lambda i: (0, 0)),
        ],
        out_specs=pl.BlockSpec((tm, LANES), lambda i: (i, 0)),
        out_shape=jax.ShapeDtypeStruct((T, LANES), F32),
        compiler_params=_params(("parallel",), blocks),
        name="gdn_gates",
    )(small, alog_l, dtb_l)


def _window_masks(reverse):
    w = GDN_WIN
    ri = lax.broadcasted_iota(jnp.int32, (w, w), 0)
    ci = lax.broadcasted_iota(jnp.int32, (w, w), 1)
    same = jnp.right_shift(ri, 6) == jnp.right_shift(ci, 6)
    order_incl = (ri <= ci) if reverse else (ri >= ci)
    order_strict = (ri < ci) if reverse else (ri > ci)
    return dict(
        incl=jnp.logical_and(same, order_incl),
        strict=jnp.logical_and(same, order_strict),
        b16=jnp.right_shift(ri, 4) == jnp.right_shift(ci, 4),
        b32=jnp.right_shift(ri, 5) == jnp.right_shift(ci, 5),
        diag=ri == ci,
    )


def _delta_prep_stages(chains, out):
    w = GDN_WIN

    def bf(x):
        return x.astype(BF16)

    def cat(x, y):
        return jnp.concatenate([bf(x), bf(y)], axis=1)

    n = range(len(chains))
    gram = [_mm_nt(jnp.concatenate([c["q"], c["k"]], axis=0), c["k"]) for c in chains]
    yield
    decay = [jnp.exp(jnp.where(c["m"]["incl"], c["gc"] - c["gr"], -jnp.inf)) for c in chains]
    a = [jnp.where(chains[i]["m"]["strict"], gram[i][w:] * chains[i]["bc"] * decay[i], 0.0) for i in n]
    nd = [jnp.where(chains[i]["m"]["b16"], -a[i], 0.0) for i in n]
    n2 = [_mm(bf(nd[i]), bf(nd[i])) for i in n]
    yield
    p = [jnp.where(chains[i]["m"]["diag"], 1.0, nd[i]) for i in n]
    r = [_mm(bf(n2[i]), cat(p[i], n2[i])) for i in n]
    yield
    p = [p[i] + r[i][:, :w] for i in n]
    n4 = [r[i][:, w:] for i in n]
    r = [_mm(bf(n4[i]), cat(p[i], n4[i])) for i in n]
    yield
    p = [p[i] + r[i][:, :w] for i in n]
    n8 = [r[i][:, w:] for i in n]
    r = [_mm(bf(n8[i]), bf(p[i])) for i in n]
    yield
    di = [p[i] + r[i] for i in n]
    lo = [jnp.where(chains[i]["m"]["b16"], 0.0, a[i]) for i in n]
    nt = [-_mm(bf(di[i]), bf(lo[i])) for i in n]
    yield
    r = [_mm(bf(nt[i]), cat(nt[i], di[i])) for i in n]
    yield
    s1 = [di[i] + r[i][:, w:] for i in n]
    t = [s1[i] + _mm(bf(r[i][:, :w]), bf(s1[i])) for i in n]
    yield
    eg = [jnp.exp(c["gc"]) for c in chains]
    kf = [c["k"].astype(F32) for c in chains]
    rhs = [cat(kf[i] * (chains[i]["bc"] * eg[i]), chains[i]["v"].astype(F32) * chains[i]["bc"]) for i in n]
    wu = [bf(_mm(bf(t[i]), rhs[i])) for i in n]
    yield
    col_chunk = jnp.right_shift(lax.broadcasted_iota(jnp.int32, (GDN_DK, w), 1), 6)
    lhs = []
    for i in n:
        c = chains[i]
        qk = jnp.where(c["m"]["incl"], gram[i][:w] * decay[i], 0.0)
        k_end_t = jnp.transpose(kf[i] * jnp.exp(c["glc"] - c["gc"]))
        spread = [jnp.where(col_chunk == j, k_end_t, 0.0) for j in range(w // CHUNK)]
        lhs.append(bf(jnp.concatenate([qk] + spread, axis=0)))
    res = [_mm(lhs[i], wu[i]) for i in n]
    yield
    for i in n:
        c = chains[i]
        x = res[i][:w]
        qp = c["q"].astype(F32) * eg[i] - x[:, :GDN_DK]
        out.append((bf(qp), x[:, GDN_DK:], bf(res[i][w:])))


def _gdn_kernel(q_ref, k_ref, v_ref, gate_ref, col_ref, row_ref, ng_ref, o_ref,
                st_ref, of_ref, ob_ref, qp_ref, y_ref, *, S, hg):
    nwin = S // GDN_WIN
    cpw = GDN_WIN // CHUNK
    st_ref[...] = jnp.zeros(st_ref.shape, F32)

    def chains():
        for d in range(2):
            for hl in range(hg):
                yield d, hl, d * hg + hl

    def win_start(w, d):
        wi = w if d == 0 else nwin - 1 - w
        return wi, pl.multiple_of(wi * GDN_WIN, GDN_WIN)

    def prep_stages(w, slot):
        masks = (_window_masks(False), _window_masks(True))
        ins = []
        for d, hl, ch in chains():
            wi, r0 = win_start(w, d)
            hs = slice(hl * LANES, (hl + 1) * LANES)
            colv = col_ref[pl.ds(r0, GDN_WIN), :]
            rowv = row_ref[wi]
            gi, bi, li = ch, 2 * hg + ch, 4 * hg + ch
            ins.append(dict(q=q_ref[pl.ds(r0, GDN_WIN), hs], k=k_ref[pl.ds(r0, GDN_WIN), hs],
                            v=v_ref[pl.ds(r0, GDN_WIN), hs], gc=colv[:, gi:gi + 1], gr=rowv[gi:gi + 1, :],
                            bc=colv[:, bi:bi + 1], glc=colv[:, li:li + 1], m=masks[d]))
        out = []
        yield from _delta_prep_stages(ins, out)
        for (d, hl, ch), (qp, o0, y) in zip(chains(), out):
            _, r0 = win_start(w, d)
            qp_ref[slot, ch] = qp
            y_ref[slot, ch] = y
            (of_ref if d == 0 else ob_ref)[pl.ds(r0, GDN_WIN), pl.ds(hl * LANES, LANES)] = o0

    def seq_stages(w, slot):
        states = [st_ref[ch] for _, _, ch in chains()]
        for step in range(cpw):
            for i, (d, hl, ch) in enumerate(chains()):
                _, r0 = win_start(w, d)
                hs = slice(hl * LANES, (hl + 1) * LANES)
                dst = of_ref if d == 0 else ob_ref
                j = step if d == 0 else cpw - 1 - step
                rj = pl.multiple_of(r0 + j * CHUNK, CHUNK)
                yj = y_ref[slot, ch, j * GDN_DK:(j + 1) * GDN_DK, :]
                lhs = jnp.concatenate([yj[:, :GDN_DK], qp_ref[slot, ch, j * CHUNK:(j + 1) * CHUNK, :]], axis=0)
                r = _mm(lhs, states[i].astype(BF16))
                glv = col_ref[pl.ds(rj, 1), :][:, 4 * hg + ch:4 * hg + ch + 1]
                dst[pl.ds(rj, CHUNK), hs] = dst[pl.ds(rj, CHUNK), hs] + r[GDN_DK:]
                states[i] = states[i] * jnp.exp(glv) - r[:GDN_DK] + yj[:, GDN_DK:].astype(F32)
            yield
        for i, (_, _, ch) in enumerate(chains()):
            st_ref[ch] = states[i]

    def run(*gens):
        live = list(gens)
        while live:
            for g in list(live):
                try:
                    next(g)
                except StopIteration:
                    live.remove(g)

    run(prep_stages(0, 0))

    def body(w, carry):
        slot = lax.rem(w, 2)
        run(seq_stages(w, slot), prep_stages(w + 1, 1 - slot))
        return carry

    lax.fori_loop(0, nwin - 1, body, 0)
    run(seq_stages(nwin - 1, (nwin - 1) % 2))

    rows = 256
    for t in range(S // rows):
        rs = slice(t * rows, (t + 1) * rows)
        for hl in range(hg):
            hs = slice(hl * LANES, (hl + 1) * LANES)
            o = of_ref[rs, hs] + ob_ref[rs, hs]
            ms = jnp.mean(o * o, axis=-1, keepdims=True)
            gate = gate_ref[rs, hs].astype(F32)
            y = o * lax.rsqrt(ms + EPS) * ng_ref[...]
            o_ref[rs, hs] = (y * (gate * jax.nn.sigmoid(gate))).astype(o_ref.dtype)


def gdn_scan(qkvc, big, gate_col_block0, col, row, norm_g, *, B, S):
    T = B * S
    hg = GDN_HG
    ngroups = GDN_HEADS // hg
    w = hg * LANES
    nwin = S // GDN_WIN
    cpw = GDN_WIN // CHUNK
    assert S % GDN_WIN == 0 and nwin >= 2
    blocks = [5 * _nbytes((S, w), BF16), _nbytes((S, LANES), F32), _nbytes((nwin, 8, GDN_WIN), F32)]
    scratch = (_nbytes((2 * hg, GDN_DK, GDN_DV), F32) + 2 * _nbytes((S, w), F32)
               + _nbytes((2, 2 * hg, GDN_WIN, GDN_DK), BF16) + _nbytes((2, 2 * hg, cpw * GDN_DK, 2 * GDN_DV), BF16))
    return pl.pallas_call(
        functools.partial(_gdn_kernel, S=S, hg=hg),
        grid=(B, ngroups),
        in_specs=[
            pl.BlockSpec((S, w), lambda b, g: (b, g)),
            pl.BlockSpec((S, w), lambda b, g: (b, ngroups + g)),
            pl.BlockSpec((S, w), lambda b, g: (b, 2 * ngroups + g)),
            pl.BlockSpec((S, w), lambda b, g: (b, gate_col_block0 + g)),
            pl.BlockSpec((None, None, S, 6 * hg), lambda b, g: (b, g, 0, 0)),
            pl.BlockSpec((None, None, nwin, 2 * hg, GDN_WIN), lambda b, g: (b, g, 0, 0, 0)),
            pl.BlockSpec((1, GDN_DV), lambda b, g: (0, 0)),
        ],
        out_specs=pl.BlockSpec((S, w), lambda b, g: (b, g)),
        out_shape=jax.ShapeDtypeStruct((T, GDN_HEADS * GDN_DV), BF16),
        scratch_shapes=[
            pltpu.VMEM((2 * hg, GDN_DK, GDN_DV), F32),
            pltpu.VMEM((S, w), F32),
            pltpu.VMEM((S, w), F32),
            pltpu.VMEM((2, 2 * hg, GDN_WIN, GDN_DK), BF16),
            pltpu.VMEM((2, 2 * hg, cpw * GDN_DK, 2 * GDN_DV), BF16),
        ],
        compiler_params=_params(("parallel", "parallel"), blocks, scratch),
        name="gdn_scan",
    )(qkvc, qkvc, qkvc, big, col, row, norm_g)


def _router_kernel(h_ref, g_ref, rw_ref, rb_ref, n_ref, idx_ref, gate_ref):
    x = h_ref[...]
    ms = jnp.mean(x * x, axis=-1, keepdims=True)
    n = x * lax.rsqrt(ms + EPS) * g_ref[...]
    n_ref[...] = n
    logits = jnp.dot(n, rw_ref[...], precision=HIGHEST, preferred_element_type=F32) + rb_ref[...]
    lane = lax.broadcasted_iota(jnp.int32, logits.shape, 1).astype(F32)
    neg = -jnp.inf
    l = jnp.where(lane < N_EXPERTS, logits, neg)
    vals, idxs = [], []
    for _ in range(TOP_K):
        m = jnp.max(l, axis=-1, keepdims=True)
        idx = jnp.min(jnp.where(l == m, lane, float(LANES)), axis=-1, keepdims=True)
        vals.append(m)
        idxs.append(idx)
        l = jnp.where(lane == idx, neg, l)
    es = [jnp.exp(v - vals[0]) for v in vals]
    tot = es[0] + es[1] + es[2] + es[3]
    iout = jnp.zeros(logits.shape, F32)
    gout = jnp.zeros(logits.shape, F32)
    for k in range(TOP_K):
        iout = jnp.where(lane == k, idxs[k], iout)
        gout = jnp.where(lane == k, es[k] / tot, gout)
    idx_ref[...] = iout.astype(jnp.int32)
    gate_ref[...] = gout


def moe_router(h, g, rw, rb, *, tm):
    T, D = h.shape
    blocks = [2 * _nbytes((tm, D), F32), _nbytes((D, LANES), F32), 2 * _nbytes((tm, LANES), F32)]
    return pl.pallas_call(
        _router_kernel,
        grid=(T // tm,),
        in_specs=[
            pl.BlockSpec((tm, D), lambda i: (i, 0)),
            pl.BlockSpec((1, D), lambda i: (0, 0)),
            pl.BlockSpec((D, LANES), lambda i: (0, 0)),
            pl.BlockSpec((1, LANES), lambda i: (0, 0)),
        ],
        out_specs=[
            pl.BlockSpec((tm, D), lambda i: (i, 0)),
            pl.BlockSpec((tm, LANES), lambda i: (i, 0)),
            pl.BlockSpec((tm, LANES), lambda i: (i, 0)),
        ],
        out_shape=[
            jax.ShapeDtypeStruct((T, D), F32),
            jax.ShapeDtypeStruct((T, LANES), jnp.int32),
            jax.ShapeDtypeStruct((T, LANES), F32),
        ],
        compiler_params=_params(("parallel",), blocks, 4 * _nbytes((tm, D), F32)),
        name="moe_router",
    )(h, g, rw, rb)


def _row_copy(src_hbm, row, dst_ref, dst_row, sem):
    return pltpu.make_async_copy(src_hbm.at[pl.ds(row, 1), :], dst_ref.at[pl.ds(dst_row, 1), :], sem)


def _dispatch_kernel(tok_ref, nu_ref, src_hbm, o_ref, buf_ref, sem, *, bm):
    i = pl.program_id(0)
    nu = nu_ref[0]

    def start_block(blk):
        slot = lax.rem(blk, 2)

        def issue(r, c):
            _row_copy(src_hbm, tok_ref[blk * bm + r], buf_ref.at[slot], r, sem.at[slot]).start()
            return c

        lax.fori_loop(0, bm, issue, 0, unroll=8)

    @pl.when(i == 0)
    def _():
        start_block(i)

    @pl.when(i + 1 < nu)
    def _():
        start_block(i + 1)

    @pl.when(i < nu)
    def _():
        slot = lax.rem(i, 2)
        pltpu.make_async_copy(src_hbm.at[pl.ds(0, bm), :], buf_ref.at[slot], sem.at[slot]).wait()
        o_ref[...] = buf_ref[slot].astype(o_ref.dtype)

    @pl.when(i >= nu)
    def _():
        o_ref[...] = jnp.zeros(o_ref.shape, o_ref.dtype)


def moe_dispatch(row_tok, n_used, src, *, rows, bm):
    D = src.shape[1]
    blocks = [_nbytes((bm, D), BF16)]
    return pl.pallas_call(
        functools.partial(_dispatch_kernel, bm=bm),
        grid_spec=pltpu.PrefetchScalarGridSpec(
            num_scalar_prefetch=2,
            grid=(rows // bm,),
            in_specs=[pl.BlockSpec(memory_space=pl.ANY)],
            out_specs=pl.BlockSpec((bm, D), lambda i, tok, nu: (i, 0)),
            scratch_shapes=[pltpu.VMEM((2, bm, D), F32), pltpu.SemaphoreType.DMA((2,))],
        ),
        out_shape=jax.ShapeDtypeStruct((rows, D), BF16),
        compiler_params=_params(("arbitrary",), blocks, _nbytes((2, bm, D), F32)),
        name="moe_dispatch",
    )(row_tok, n_used, src)


def _expert_up_kernel(be_ref, nu_ref, x_ref, wg_ref, wu_ref, bg_ref, bu_ref, o_ref):
    i = pl.program_id(1)

    @pl.when(i < nu_ref[0])
    def _():
        x = x_ref[...]
        gate = _mm(x, wg_ref[...]) + bg_ref[...]
        up = _mm(x, wu_ref[...]) + bu_ref[...]
        gate = jnp.minimum(gate, SWIGLU_LIMIT)
        up = jnp.clip(up, -SWIGLU_LIMIT, SWIGLU_LIMIT)
        act = (up + 1.0) * gate * jax.nn.sigmoid(SWIGLU_ALPHA * gate)
        o_ref[...] = act.astype(o_ref.dtype)

    @pl.when(i >= nu_ref[0])
    def _():
        o_ref[...] = jnp.zeros(o_ref.shape, o_ref.dtype)


def moe_expert_up(block_expert, n_used, xs, w_gu, b_gu, *, bm, tn):
    R, D = xs.shape
    dff = w_gu.shape[2] // 2
    tn = min(tn, dff)
    nj = dff // tn
    nb = R // bm

    def last(i, nu):
        return jnp.minimum(i, nu[0] - 1)

    blocks = [_nbytes((bm, D), BF16), 2 * _nbytes((D, tn), BF16), 2 * _nbytes((8, tn), F32), _nbytes((bm, tn), BF16)]
    return pl.pallas_call(
        _expert_up_kernel,
        grid_spec=pltpu.PrefetchScalarGridSpec(
            num_scalar_prefetch=2,
            grid=(nj, nb),
            in_specs=[
                pl.BlockSpec((bm, D), lambda j, i, be, nu: (last(i, nu), 0)),
                pl.BlockSpec((None, D, tn), lambda j, i, be, nu: (be[last(i, nu)], 0, j)),
                pl.BlockSpec((None, D, tn), lambda j, i, be, nu: (be[last(i, nu)], 0, nj + j)),
                pl.BlockSpec((None, 1, tn), lambda j, i, be, nu: (be[last(i, nu)], 0, j)),
                pl.BlockSpec((None, 1, tn), lambda j, i, be, nu: (be[last(i, nu)], 0, nj + j)),
            ],
            out_specs=pl.BlockSpec((bm, tn), lambda j, i, be, nu: (i, j)),
        ),
        out_shape=jax.ShapeDtypeStruct((R, dff), BF16),
        compiler_params=_params(("arbitrary", "arbitrary"), blocks, 4 * _nbytes((bm, tn), F32)),
        name="moe_expert_up",
    )(block_expert, n_used, xs, w_gu, w_gu, b_gu, b_gu)


def _expert_down_kernel(be_ref, nu_ref, a_ref, w_ref, b_ref, o_ref):
    i = pl.program_id(1)

    @pl.when(i < nu_ref[0])
    def _():
        o_ref[...] = _mm(a_ref[...], w_ref[...]) + b_ref[...]

    @pl.when(i >= nu_ref[0])
    def _():
        o_ref[...] = jnp.zeros(o_ref.shape, o_ref.dtype)


def moe_expert_down(block_expert, n_used, act, w_down, b_down, *, bm, tn):
    R, dff = act.shape
    D = w_down.shape[2]
    tn = min(tn, D)

    def last(i, nu):
        return jnp.minimum(i, nu[0] - 1)

    blocks = [_nbytes((bm, dff), BF16), _nbytes((dff, tn), BF16), _nbytes((8, tn), F32), _nbytes((bm, tn), F32)]
    return pl.pallas_call(
        _expert_down_kernel,
        grid_spec=pltpu.PrefetchScalarGridSpec(
            num_scalar_prefetch=2,
            grid=(D // tn, R // bm),
            in_specs=[
                pl.BlockSpec((bm, dff), lambda j, i, be, nu: (last(i, nu), 0)),
                pl.BlockSpec((None, dff, tn), lambda j, i, be, nu: (be[last(i, nu)], 0, j)),
                pl.BlockSpec((None, 1, tn), lambda j, i, be, nu: (be[last(i, nu)], 0, j)),
            ],
            out_specs=pl.BlockSpec((bm, tn), lambda j, i, be, nu: (i, j)),
        ),
        out_shape=jax.ShapeDtypeStruct((R, D), F32),
        compiler_params=_params(("arbitrary", "arbitrary"), blocks, _nbytes((bm, tn), F32)),
        name="moe_expert_down",
    )(block_expert, n_used, act, w_down, b_down)


def _combine_kernel(dest_ref, h_ref, gt_ref, y_hbm, g_ref, o_ref, buf_ref, sem, *, tb, nblocks):
    i = pl.program_id(0)

    def start_block(blk):
        slot = lax.rem(blk, 2)

        def issue(r, c):
            for k in range(TOP_K):
                _row_copy(y_hbm, dest_ref[(blk * tb + r) * TOP_K + k], buf_ref.at[slot, k], r, sem.at[slot]).start()
            return c

        lax.fori_loop(0, tb, issue, 0, unroll=4)

    @pl.when(i == 0)
    def _():
        start_block(i)

    @pl.when(i + 1 < nblocks)
    def _():
        start_block(i + 1)

    slot = lax.rem(i, 2)
    for k in range(TOP_K):
        pltpu.make_async_copy(y_hbm.at[pl.ds(0, tb), :], buf_ref.at[slot, k], sem.at[slot]).wait()

    def body(r, c):
        sl = pl.ds(pl.multiple_of(r * NORM_ROWS, NORM_ROWS), NORM_ROWS)
        hv = h_ref[sl, :]
        gt = gt_ref[sl, :]
        for k in range(TOP_K):
            hv = hv + buf_ref[slot, k, sl, :] * gt[:, k:k + 1]
        ms = jnp.mean(hv * hv, axis=-1, keepdims=True)
        o_ref[sl, :] = hv * lax.rsqrt(ms + EPS) * g_ref[...]
        return c

    lax.fori_loop(0, tb // NORM_ROWS, body, 0)


def moe_combine(dest, h, top_gate, y, g, *, tb):
    T, D = h.shape
    blocks = [2 * _nbytes((tb, D), F32), _nbytes((tb, LANES), F32)]
    return pl.pallas_call(
        functools.partial(_combine_kernel, tb=tb, nblocks=T // tb),
        grid_spec=pltpu.PrefetchScalarGridSpec(
            num_scalar_prefetch=1,
            grid=(T // tb,),
            in_specs=[
                pl.BlockSpec((tb, D), lambda i, d: (i, 0)),
                pl.BlockSpec((tb, LANES), lambda i, d: (i, 0)),
                pl.BlockSpec(memory_space=pl.ANY),
                pl.BlockSpec((1, D), lambda i, d: (0, 0)),
            ],
            out_specs=pl.BlockSpec((tb, D), lambda i, d: (i, 0)),
            scratch_shapes=[pltpu.VMEM((2, TOP_K, tb, D), F32), pltpu.SemaphoreType.DMA((2,))],
        ),
        out_shape=jax.ShapeDtypeStruct((T, D), F32),
        compiler_params=_params(("arbitrary",), blocks, _nbytes((2, TOP_K, tb, D), F32)),
        name="moe_combine",
    )(dest, h, top_gate, y, g)


def _routing_tables(top_idx, *, bm):
    T = top_idx.shape[0]
    tk = T * TOP_K
    flat_e = top_idx.reshape(tk)
    onehot = (flat_e[:, None] == jnp.arange(N_EXPERTS, dtype=jnp.int32)[None, :]).astype(jnp.int32)
    csum = jnp.cumsum(onehot, axis=0)
    rank = jnp.take_along_axis(csum, flat_e[:, None], axis=1)[:, 0] - 1
    counts = csum[-1]
    padded = (counts + bm - 1) // bm * bm
    pad_end = jnp.cumsum(padded)
    pad_start = pad_end - padded
    dest = (pad_start[flat_e] + rank).astype(jnp.int32)
    nb = tk // bm + N_EXPERTS
    rows = nb * bm
    flat_tok = jnp.repeat(jnp.arange(T, dtype=jnp.int32), TOP_K)
    row_tok = jnp.zeros((rows,), jnp.int32).at[dest].set(flat_tok)
    block_expert = jnp.minimum(
        jnp.searchsorted(pad_end, jnp.arange(nb, dtype=pad_end.dtype) * bm, side="right"), N_EXPERTS - 1
    ).astype(jnp.int32)
    n_used = (pad_end[-1] // bm).astype(jnp.int32).reshape(1)
    return dest, row_tok, block_expert, n_used, rows


def _rot_half(w):
    half = QK_ROPE // 2
    return jnp.concatenate([-w[..., half:], w[..., :half]], axis=-1)


def _layer(h, pos, invf, ln1_g, w_in, q_norm_g, w_q_up, kv_norm_g, w_kv_up, conv_w, a_log, dt_bias, gdn_norm_g,
           w_mla_o, w_gdn_o, w_out, ln2_g, router_w, router_b, w_gu, b_gu, w_down, b_down, out_g, *, B, S):
    T, D = h.shape
    H = MLA_HEADS
    gqkv = 2 * GDN_HEADS * GDN_DK + GDN_HEADS * GDN_DV
    gw = GDN_HEADS * GDN_DV
    o0 = Q_LORA + KV_LORA
    c_kr, c_qkv = o0, o0 + QK_ROPE
    c_gate = c_qkv + gqkv
    c_ab = c_gate + gw
    c_za = c_ab + 4 * GDN_HEADS
    c_zb = c_za + D

    w_kr = w_in[:, c_kr:c_qkv]
    w_kt = _rot_half(w_kr)
    small_cols = [w_in[:, :o0], w_kr, w_kr, w_kt, w_kt, w_in[:, c_ab:c_za]]
    n_small = 2048
    used = o0 + 4 * QK_ROPE + 4 * GDN_HEADS
    small_cols.append(jnp.zeros((D, n_small - used), F32))
    w_small = jnp.concatenate(small_cols, axis=1).astype(BF16)
    w_big = jnp.concatenate([w_in[:, c_qkv:c_gate], w_in[:, c_gate:c_ab], w_in[:, c_za:c_zb], w_in[:, c_zb:]],
                            axis=1).astype(BF16)
    ka_blk = o0 // LANES
    kb_blk = ka_blk + 1
    ab_blk = ka_blk + 2

    wq = w_q_up.reshape(Q_LORA, H, QK_NOPE + QK_ROPE)
    wq_rope = wq[:, :, QK_NOPE:]
    wq2 = jnp.concatenate([wq[:, :, :QK_NOPE], wq_rope, _rot_half(wq_rope)], axis=-1).reshape(Q_LORA, H * 2 * LANES)
    wq2 = wq2.astype(BF16)
    wkv = w_kv_up.reshape(KV_LORA, H, QK_NOPE + V_HEAD)
    wkv2 = jnp.concatenate([wkv[:, :, :QK_NOPE].reshape(KV_LORA, H * QK_NOPE),
                            wkv[:, :, QK_NOPE:].reshape(KV_LORA, H * V_HEAD)], axis=1).astype(BF16)

    small = norm_matmul(h, ln1_g, w_small, out_dtype=F32, tm=512, tn=1024, name="in_proj_small")
    big = norm_matmul(h, ln1_g, w_big, out_dtype=BF16, tm=512, tn=1024, name="in_proj_big")

    q2 = norm_matmul(small, q_norm_g, wq2, out_dtype=BF16, tm=1024, tn=1024, x_col_block=0, name="q_up")
    kv = norm_matmul(small, kv_norm_g, wkv2, out_dtype=BF16, tm=1024, tn=1024, x_col_block=Q_LORA // KV_LORA,
                     name="kv_up")
    score_scale = (QK_NOPE + QK_ROPE) ** -0.5 * LOG2_E
    kr2, cs = rope_prep(pos, invf, small, ka_blk, kb_blk, tm=1024, scale=score_scale)
    o_a = mla_attention(q2, cs, kv, kr2, B=B, S=S, tq=256, scale=score_scale)

    qkvc = gdn_conv(big, conv_w, B=B, S=S)
    pad = jnp.zeros((LANES - 2 * GDN_HEADS,), F32)
    alog_l = jnp.concatenate([a_log.reshape(-1), pad]).reshape(1, LANES)
    dtb_l = jnp.concatenate([dt_bias.reshape(-1), pad]).reshape(1, LANES)
    gp = gdn_gates(small, ab_blk, alog_l, dtb_l, tm=512)
    hg = GDN_HG
    ng = GDN_HEADS // hg
    nch = S // CHUNK
    nwin = S // GDN_WIN
    gp4 = gp[:, :4 * GDN_HEADS].reshape(B, S, 4, ng, hg)
    gcum = gp4[:, :, :2].reshape(B, nch, CHUNK, 2, ng, hg)
    g_end = jnp.stack([gcum[:, :, CHUNK - 1, 0], gcum[:, :, 0, 1]], axis=2)
    g_end = jnp.broadcast_to(g_end[:, :, None], (B, nch, CHUNK, 2, ng, hg)).reshape(B, S, 2, ng, hg)
    col = jnp.concatenate([gp4, g_end], axis=2).transpose(0, 3, 1, 2, 4).reshape(B, ng, S, 6 * hg)
    row = gp4[:, :, :2].reshape(B, nwin, GDN_WIN, 2, ng, hg).transpose(0, 4, 1, 3, 5, 2)
    row = row.reshape(B, ng, nwin, 2 * hg, GDN_WIN)
    o_b = gdn_scan(qkvc, big, gqkv // (hg * LANES), col, row, gdn_norm_g.reshape(1, GDN_DV), B=B, S=S)

    merged = gated_pair_matmul(o_a, w_mla_o.astype(BF16), o_b, w_gdn_o.astype(BF16), big,
                               (gqkv + gw) // 1024, (gqkv + gw + D) // 1024, tm=512, tn=1024, name="mixer_out")
    h1 = residual_matmul(merged, w_out.astype(BF16), h, tm=512, tn=1024, name="out_proj")

    rw = jnp.concatenate([router_w, jnp.zeros((D, LANES - N_EXPERTS), F32)], axis=1)
    rb = jnp.concatenate([router_b, jnp.zeros((LANES - N_EXPERTS,), F32)]).reshape(1, LANES)
    n2, top_idx, top_gate = moe_router(h1, ln2_g, rw, rb, tm=256)
    dest, row_tok, block_expert, n_used, rows = _routing_tables(top_idx[:, :TOP_K], bm=MOE_BM)
    xs = moe_dispatch(row_tok, n_used, n2, rows=rows, bm=MOE_BM)
    dff = w_gu.shape[2] // 2
    act = moe_expert_up(block_expert, n_used, xs, w_gu.astype(BF16), b_gu.reshape(N_EXPERTS, 1, 2 * dff),
                        bm=MOE_BM, tn=1024)
    y = moe_expert_down(block_expert, n_used, act, w_down.astype(BF16), b_down.reshape(N_EXPERTS, 1, D),
                        bm=MOE_BM, tn=2048)
    return moe_combine(dest, h1, top_gate, y, out_g, tb=128)


def kernel(x, positions, ln1_g, w_in, q_norm_g, w_q_up, kv_norm_g, w_kv_up, gdn_conv_w, gdn_a_log, gdn_dt_bias,
           gdn_norm_g, w_mla_o, w_gdn_o, w_out, ln2_g, router_w, router_b, w_gu, b_gu, w_down, b_down, ln_f_g):
    B, S, D = x.shape
    depth = ln1_g.shape[0]
    assert depth == 1, "the final norm is fused into the last layer's expert combine"
    half = QK_ROPE // 2
    inv_freq = ROPE_THETA ** (-jnp.arange(0, QK_ROPE, 2, dtype=F32) / QK_ROPE)
    invf = jnp.tile(inv_freq, LANES // half).reshape(1, LANES)
    pos = positions.reshape(B * S, 1).astype(jnp.int32)
    h = x.reshape(B * S, D)
    l = 0
    out = _layer(h, pos, invf, ln1_g[l].reshape(1, D), w_in[l], q_norm_g[l].reshape(1, Q_LORA), w_q_up[l],
                 kv_norm_g[l].reshape(1, KV_LORA), w_kv_up[l], gdn_conv_w[l], gdn_a_log[l], gdn_dt_bias[l],
                 gdn_norm_g[l], w_mla_o[l], w_gdn_o[l], w_out[l], ln2_g[l].reshape(1, D), router_w[l], router_b[l],
                 w_gu[l], b_gu[l], w_down[l], b_down[l], ln_f_g.reshape(1, D), B=B, S=S)
    return out.reshape(B, S, D)
```

```python
import functools

import jax
import jax.numpy as jnp
from jax import lax
from jax.experimental import pallas as pl
from jax.experimental.pallas import tpu as pltpu

F32 = jnp.float32
BF16 = jnp.bfloat16

EPS = 1e-6
MLA_HEADS = 16
Q_LORA = 1024
KV_LORA = 512
QK_NOPE = 128
QK_ROPE = 64
V_HEAD = 128
ROPE_THETA = 10000.0
GDN_HEADS = 16
GDN_DK = 128
GDN_DV = 128
CONV_W = 5
CHUNK = 64
N_EXPERTS = 32
TOP_K = 4
SWIGLU_LIMIT = 7.0
SWIGLU_ALPHA = 1.702

LANES = 128
V7X_VMEM_BYTES = 64 * 1024 * 1024
VMEM_CAP = V7X_VMEM_BYTES - 8 * 1024 * 1024
LOG2_E = 1.4426950408889634
ATTN_HEADS_PER_STEP = 2
NORM_ROWS = 32
GDN_HG = 2
GDN_WIN = 256
MOE_BM = 256
HIGHEST = lax.Precision.HIGHEST


def _nbytes(shape, dtype):
    n = 1
    for s in shape:
        n *= s
    return n * jnp.dtype(dtype).itemsize


def _params(sem, blocks, scratch=0):
    need = 2 * sum(blocks) + scratch + 16 * 1024 * 1024
    return pltpu.CompilerParams(dimension_semantics=sem, vmem_limit_bytes=int(min(need, VMEM_CAP)))


def _mm(a, b):
    return jnp.dot(a, b, preferred_element_type=F32)


def _mm_nt(a, b):
    return lax.dot_general(a, b, (((1,), (1,)), ((), ())), preferred_element_type=F32)


def _mm_tn(a, b):
    return lax.dot_general(a, b, (((0,), (0,)), ((), ())), preferred_element_type=F32)


def _norm_rows_into(x_ref, g_ref, xn_ref, rows):
    def body(r, c):
        sl = pl.ds(pl.multiple_of(r * NORM_ROWS, NORM_ROWS), NORM_ROWS)
        xv = x_ref[sl, :].astype(F32)
        ms = jnp.mean(xv * xv, axis=-1, keepdims=True)
        xn_ref[sl, :] = (xv * lax.rsqrt(ms + EPS) * g_ref[...]).astype(BF16)
        return c

    lax.fori_loop(0, rows // NORM_ROWS, body, 0)


def _norm_matmul_kernel(x_ref, g_ref, w_ref, o_ref, xn_ref, *, tm):
    @pl.when(pl.program_id(1) == 0)
    def _():
        _norm_rows_into(x_ref, g_ref, xn_ref, tm)

    o_ref[...] = _mm(xn_ref[...], w_ref[...]).astype(o_ref.dtype)


def norm_matmul(x, g, w, *, out_dtype, tm, tn, x_col_block=0, name):
    T = x.shape[0]
    K, N = w.shape
    tm = min(tm, T)
    assert T % tm == 0 and N % tn == 0 and tm % NORM_ROWS == 0
    blocks = [_nbytes((tm, K), x.dtype), _nbytes((K, tn), BF16), _nbytes((tm, tn), out_dtype)]
    return pl.pallas_call(
        functools.partial(_norm_matmul_kernel, tm=tm),
        grid=(T // tm, N // tn),
        in_specs=[
            pl.BlockSpec((tm, K), lambda i, j: (i, x_col_block)),
            pl.BlockSpec((1, K), lambda i, j: (0, 0)),
            pl.BlockSpec((K, tn), lambda i, j: (0, j)),
        ],
        out_specs=pl.BlockSpec((tm, tn), lambda i, j: (i, j)),
        out_shape=jax.ShapeDtypeStruct((T, N), out_dtype),
        scratch_shapes=[pltpu.VMEM((tm, K), BF16)],
        compiler_params=_params(("parallel", "arbitrary"), blocks, _nbytes((tm, K), BF16)),
        name=name,
    )(x, g, w)


def _gated_pair_kernel(a_ref, wa_ref, za_ref, b_ref, wb_ref, zb_ref, o_ref):
    ya = _mm(a_ref[...], wa_ref[...])
    yb = _mm(b_ref[...], wb_ref[...])
    ga = jax.nn.sigmoid(za_ref[...].astype(F32))
    gb = jax.nn.sigmoid(zb_ref[...].astype(F32))
    o_ref[...] = (ga * ya + gb * yb).astype(o_ref.dtype)


def gated_pair_matmul(a, wa, b, wb, z, za_col_block, zb_col_block, *, tm, tn, name):
    T, K = a.shape
    N = wa.shape[1]
    blocks = [2 * _nbytes((tm, K), BF16), 2 * _nbytes((K, tn), BF16), 3 * _nbytes((tm, tn), BF16)]
    return pl.pallas_call(
        _gated_pair_kernel,
        grid=(T // tm, N // tn),
        in_specs=[
            pl.BlockSpec((tm, K), lambda i, j: (i, 0)),
            pl.BlockSpec((K, tn), lambda i, j: (0, j)),
            pl.BlockSpec((tm, tn), lambda i, j: (i, za_col_block + j)),
            pl.BlockSpec((tm, K), lambda i, j: (i, 0)),
            pl.BlockSpec((K, tn), lambda i, j: (0, j)),
            pl.BlockSpec((tm, tn), lambda i, j: (i, zb_col_block + j)),
        ],
        out_specs=pl.BlockSpec((tm, tn), lambda i, j: (i, j)),
        out_shape=jax.ShapeDtypeStruct((T, N), BF16),
        compiler_params=_params(("parallel", "parallel"), blocks),
        name=name,
    )(a, wa, z, b, wb, z)


def _residual_matmul_kernel(x_ref, w_ref, r_ref, o_ref):
    o_ref[...] = r_ref[...] + _mm(x_ref[...], w_ref[...])


def residual_matmul(x, w, r, *, tm, tn, name):
    T, K = x.shape
    N = w.shape[1]
    blocks = [_nbytes((tm, K), BF16), _nbytes((K, tn), BF16), 2 * _nbytes((tm, tn), F32)]
    return pl.pallas_call(
        _residual_matmul_kernel,
        grid=(T // tm, N // tn),
        in_specs=[
            pl.BlockSpec((tm, K), lambda i, j: (i, 0)),
            pl.BlockSpec((K, tn), lambda i, j: (0, j)),
            pl.BlockSpec((tm, tn), lambda i, j: (i, j)),
        ],
        out_specs=pl.BlockSpec((tm, tn), lambda i, j: (i, j)),
        out_shape=jax.ShapeDtypeStruct((T, N), F32),
        compiler_params=_params(("parallel", "parallel"), blocks),
        name=name,
    )(x, w, r)


def _rope_prep_kernel(pos_ref, invf_ref, ka_ref, kb_ref, kr_ref, cs_ref, *, scale):
    ang = pos_ref[...].astype(F32) * invf_ref[...]
    c = jnp.cos(ang)
    s = jnp.sin(ang)
    kr_ref[...] = (ka_ref[...] * c + kb_ref[...] * s).astype(kr_ref.dtype)
    lane = lax.broadcasted_iota(jnp.int32, c.shape, 1)
    cs_ref[...] = jnp.where(lane < QK_ROPE, c, s) * scale


def rope_prep(pos, invf, small, ka_col_block, kb_col_block, *, tm, scale):
    T = pos.shape[0]
    blocks = [4 * _nbytes((tm, LANES), F32), _nbytes((tm, LANES), F32)]
    return pl.pallas_call(
        functools.partial(_rope_prep_kernel, scale=scale),
        grid=(T // tm,),
        in_specs=[
            pl.BlockSpec((tm, 1), lambda i: (i, 0)),
            pl.BlockSpec((1, LANES), lambda i: (0, 0)),
            pl.BlockSpec((tm, LANES), lambda i: (i, ka_col_block)),
            pl.BlockSpec((tm, LANES), lambda i: (i, kb_col_block)),
        ],
        out_specs=[pl.BlockSpec((tm, LANES), lambda i: (i, 0)), pl.BlockSpec((tm, LANES), lambda i: (i, 0))],
        out_shape=[jax.ShapeDtypeStruct((T, LANES), BF16), jax.ShapeDtypeStruct((T, LANES), F32)],
        compiler_params=_params(("parallel",), blocks),
        name="rope_prep",
    )(pos, invf, small, small)


def _attention_kernel(q_ref, cs_ref, kn_ref, kr_ref, v_ref, o_ref, kcat_ref, *, scale, nh):
    @pl.when(pl.program_id(2) == 0)
    def _():
        for a in range(nh):
            kcat_ref[a, :, :QK_NOPE] = kn_ref[:, a * QK_NOPE:(a + 1) * QK_NOPE]
            kcat_ref[a, :, QK_NOPE:] = kr_ref[...]

    heads = range(nh)
    qq = []
    for a in heads:
        q = q_ref[:, a * 2 * LANES:(a + 1) * 2 * LANES]
        qn = (q[:, :QK_NOPE].astype(F32) * scale).astype(BF16)
        qr = (q[:, QK_NOPE:].astype(F32) * cs_ref[...]).astype(BF16)
        qq.append(jnp.concatenate([qn, qr], axis=1))
    s = [_mm_nt(qq[a], kcat_ref[a]) for a in heads]
    m = [jnp.max(s[a], axis=-1, keepdims=True) for a in heads]
    p = [jnp.exp2(s[a] - m[a]) for a in heads]
    l = [jnp.sum(p[a], axis=-1, keepdims=True) for a in heads]
    o = [_mm(p[a].astype(BF16), v_ref[:, a * V_HEAD:(a + 1) * V_HEAD]) for a in heads]
    for a in heads:
        o_ref[:, a * V_HEAD:(a + 1) * V_HEAD] = (o[a] / l[a]).astype(o_ref.dtype)


def mla_attention(q2, cs, kv, kr2, *, B, S, tq, scale):
    T = B * S
    H = MLA_HEADS
    nh = ATTN_HEADS_PER_STEP
    nq = S // tq
    wq, wk = nh * 2 * LANES, nh * LANES
    blocks = [_nbytes((tq, wq), BF16), _nbytes((tq, LANES), F32), 2 * _nbytes((S, wk), BF16),
              _nbytes((S, LANES), BF16), _nbytes((tq, wk), BF16)]
    scratch = _nbytes((nh, S, 2 * LANES), BF16) + 3 * nh * _nbytes((tq, S), F32)
    return pl.pallas_call(
        functools.partial(_attention_kernel, scale=scale, nh=nh),
        grid=(B, H // nh, nq),
        in_specs=[
            pl.BlockSpec((tq, wq), lambda b, h, i: (b * nq + i, h)),
            pl.BlockSpec((tq, LANES), lambda b, h, i: (b * nq + i, 0)),
            pl.BlockSpec((S, wk), lambda b, h, i: (b, h)),
            pl.BlockSpec((S, LANES), lambda b, h, i: (b, 0)),
            pl.BlockSpec((S, wk), lambda b, h, i: (b, H // nh + h)),
        ],
        out_specs=pl.BlockSpec((tq, wk), lambda b, h, i: (b * nq + i, h)),
        out_shape=jax.ShapeDtypeStruct((T, H * V_HEAD), BF16),
        scratch_shapes=[pltpu.VMEM((nh, S, 2 * LANES), BF16)],
        compiler_params=_params(("parallel", "parallel", "arbitrary"), blocks, scratch),
        name="mla_attention",
    )(q2, cs, kv, kr2, kv)


def _conv_kernel(x_ref, w_ref, o_ref, xp_ref, *, S, rows, n_qk_blocks, n_q_blocks):
    pad = 8
    cb = pl.program_id(1)
    xp_ref[0:pad, :] = jnp.zeros((pad, LANES), F32)
    xp_ref[S + pad:S + 2 * pad, :] = jnp.zeros((pad, LANES), F32)
    for t in range(S // rows):
        xp_ref[pad + t * rows:pad + (t + 1) * rows, :] = x_ref[t * rows:(t + 1) * rows, :].astype(F32)
    w = w_ref[...]
    half = CONV_W // 2
    q_scale = jnp.where(cb < n_q_blocks, GDN_DK ** -0.5, 1.0).astype(F32)
    for t in range(S // rows):
        acc = jnp.zeros((rows, LANES), F32)
        for k in range(CONV_W):
            off = pad + t * rows + k - half
            acc = acc + xp_ref[off:off + rows, :] * w[k:k + 1, :]
        y = acc * jax.nn.sigmoid(acc)
        ss = jnp.sum(y * y, axis=-1, keepdims=True)
        f = jnp.where(cb < n_qk_blocks, lax.rsqrt(ss + EPS) * q_scale, 1.0)
        o_ref[t * rows:(t + 1) * rows, :] = (y * f).astype(o_ref.dtype)


def gdn_conv(big, conv_w, *, B, S):
    T = B * S
    C = conv_w.shape[1]
    ncb = C // LANES
    n_q = GDN_HEADS * GDN_DK // LANES
    blocks = [2 * _nbytes((S, LANES), BF16), _nbytes((8, LANES), F32)]
    return pl.pallas_call(
        functools.partial(_conv_kernel, S=S, rows=256, n_qk_blocks=2 * n_q, n_q_blocks=n_q),
        grid=(B, ncb),
        in_specs=[
            pl.BlockSpec((S, LANES), lambda b, c: (b, c)),
            pl.BlockSpec((CONV_W, LANES), lambda b, c: (0, c)),
        ],
        out_specs=pl.BlockSpec((S, LANES), lambda b, c: (b, c)),
        out_shape=jax.ShapeDtypeStruct((T, C), BF16),
        scratch_shapes=[pltpu.VMEM((S + 16, LANES), F32)],
        compiler_params=_params(("parallel", "parallel"), blocks, _nbytes((S + 16, LANES), F32)),
        name="gdn_conv",
    )(big, conv_w)


def _gates_kernel(ab_ref, alog_ref, dtb_ref, o_ref, *, tm):
    H = GDN_HEADS
    x = ab_ref[...]
    z = x + dtb_ref[...]
    softplus = jnp.maximum(z, 0.0) + jnp.log(1.0 + jnp.exp(-jnp.abs(z)))
    g = -jnp.exp(alog_ref[...]) * softplus
    beta = jax.nn.sigmoid(x)
    lane = lax.broadcasted_iota(jnp.int32, (CHUNK, LANES), 1)
    ri = lax.broadcasted_iota(jnp.int32, (CHUNK, CHUNK), 0)
    ci = lax.broadcasted_iota(jnp.int32, (CHUNK, CHUNK), 1)
    tril = (ri >= ci).astype(F32)
    triu = (ri <= ci).astype(F32)
    for t in range(tm // CHUNK):
        sl = slice(t * CHUNK, (t + 1) * CHUNK)
        gc = g[sl]
        pre = jnp.dot(tril, gc, precision=HIGHEST, preferred_element_type=F32)
        suf = jnp.dot(triu, gc, precision=HIGHEST, preferred_element_type=F32)
        cum = jnp.where(lane < H, pre, suf)
        o_ref[sl, :] = jnp.where(lane < 2 * H, cum, beta[sl])


def gdn_gates(small, ab_col_block, alog_l, dtb_l, *, tm):
    T = small.shape[0]
    blocks = [2 * _nbytes((tm, LANES), F32)]
    return pl.pallas_call(
        functools.partial(_gates_kernel, tm=tm),
        grid=(T // tm,),
        in_specs=[
            pl.BlockSpec((tm, LANES), lambda i: (i, ab_col_block)),
            pl.BlockSpec((1, LANES), lambda i: (0, 0)),
            pl.BlockSpec((1, LANES), lambda i: (0, 0)),
        ],
        out_specs=pl.BlockSpec((tm, LANES), lambda i: (i, 0)),
        out_shape=jax.ShapeDtypeStruct((T, LANES), F32),
        compiler_params=_params(("parallel",), blocks),
        name="gdn_gates",
    )(small, alog_l, dtb_l)


def _window_masks(reverse):
    w = GDN_WIN
    ri = lax.broadcasted_iota(jnp.int32, (w, w), 0)
    ci = lax.broadcasted_iota(jnp.int32, (w, w), 1)
    same = jnp.right_shift(ri, 6) == jnp.right_shift(ci, 6)
    order_incl = (ri <= ci) if reverse else (ri >= ci)
    order_strict = (ri < ci) if reverse else (ri > ci)
    return dict(
        incl=jnp.logical_and(same, order_incl),
        strict=jnp.logical_and(same, order_strict),
        b16=jnp.right_shift(ri, 4) == jnp.right_shift(ci, 4),
        b32=jnp.right_shift(ri, 5) == jnp.right_shift(ci, 5),
        diag=ri == ci,
    )


def _delta_prep_stages(chains, out):
    w = GDN_WIN

    def bf(x):
        return x.astype(BF16)

    def cat(x, y):
        return jnp.concatenate([bf(x), bf(y)], axis=1)

    n = range(len(chains))
    gram = [_mm_nt(jnp.concatenate([c["q"], c["k"]], axis=0), c["k"]) for c in chains]
    yield
    decay = [jnp.exp(jnp.where(c["m"]["incl"], c["gc"] - c["gr"], -jnp.inf)) for c in chains]
    a = [jnp.where(chains[i]["m"]["strict"], gram[i][w:] * chains[i]["bc"] * decay[i], 0.0) for i in n]
    nd = [jnp.where(chains[i]["m"]["b16"], -a[i], 0.0) for i in n]
    n2 = [_mm(bf(nd[i]), bf(nd[i])) for i in n]
    yield
    p = [jnp.where(chains[i]["m"]["diag"], 1.0, nd[i]) for i in n]
    r = [_mm(bf(n2[i]), cat(p[i], n2[i])) for i in n]
    yield
    p = [p[i] + r[i][:, :w] for i in n]
    n4 = [r[i][:, w:] for i in n]
    r = [_mm(bf(n4[i]), cat(p[i], n4[i])) for i in n]
    yield
    p = [p[i] + r[i][:, :w] for i in n]
    n8 = [r[i][:, w:] for i in n]
    r = [_mm(bf(n8[i]), bf(p[i])) for i in n]
    yield
    di = [p[i] + r[i] for i in n]
    lo = [jnp.where(chains[i]["m"]["b16"], 0.0, a[i]) for i in n]
    nt = [-_mm(bf(di[i]), bf(lo[i])) for i in n]
    yield
    r = [_mm(bf(nt[i]), cat(nt[i], di[i])) for i in n]
    yield
    s1 = [di[i] + r[i][:, w:] for i in n]
    t = [s1[i] + _mm(bf(r[i][:, :w]), bf(s1[i])) for i in n]
    yield
    eg = [jnp.exp(c["gc"]) for c in chains]
    kf = [c["k"].astype(F32) for c in chains]
    rhs = [cat(kf[i] * (chains[i]["bc"] * eg[i]), chains[i]["v"].astype(F32) * chains[i]["bc"]) for i in n]
    wu = [bf(_mm(bf(t[i]), rhs[i])) for i in n]
    yield
    col_chunk = jnp.right_shift(lax.broadcasted_iota(jnp.int32, (GDN_DK, w), 1), 6)
    lhs = []
    for i in n:
        c = chains[i]
        qk = jnp.where(c["m"]["incl"], gram[i][:w] * decay[i], 0.0)
        k_end_t = jnp.transpose(kf[i] * jnp.exp(c["glc"] - c["gc"]))
        spread = [jnp.where(col_chunk == j, k_end_t, 0.0) for j in range(w // CHUNK)]
        lhs.append(bf(jnp.concatenate([qk] + spread, axis=0)))
    res = [_mm(lhs[i], wu[i]) for i in n]
    yield
    for i in n:
        c = chains[i]
        x = res[i][:w]
        qp = c["q"].astype(F32) * eg[i] - x[:, :GDN_DK]
        out.append((bf(qp), x[:, GDN_DK:], bf(res[i][w:])))


def _gdn_kernel(q_ref, k_ref, v_ref, gate_ref, col_ref, row_ref, ng_ref, o_ref,
                st_ref, of_ref, ob_ref, qp_ref, y_ref, *, S, hg):
    nwin = S // GDN_WIN
    cpw = GDN_WIN // CHUNK
    st_ref[...] = jnp.zeros(st_ref.shape, F32)

    def chains():
        for d in range(2):
            for hl in range(hg):
                yield d, hl, d * hg + hl

    def win_start(w, d):
        wi = w if d == 0 else nwin - 1 - w
        return wi, pl.multiple_of(wi * GDN_WIN, GDN_WIN)

    def prep_stages(w, slot):
        masks = (_window_masks(False), _window_masks(True))
        ins = []
        for d, hl, ch in chains():
            wi, r0 = win_start(w, d)
            hs = slice(hl * LANES, (hl + 1) * LANES)
            colv = col_ref[pl.ds(r0, GDN_WIN), :]
            rowv = row_ref[wi]
            gi, bi, li = ch, 2 * hg + ch, 4 * hg + ch
            ins.append(dict(q=q_ref[pl.ds(r0, GDN_WIN), hs], k=k_ref[pl.ds(r0, GDN_WIN), hs],
                            v=v_ref[pl.ds(r0, GDN_WIN), hs], gc=colv[:, gi:gi + 1], gr=rowv[gi:gi + 1, :],
                            bc=colv[:, bi:bi + 1], glc=colv[:, li:li + 1], m=masks[d]))
        out = []
        yield from _delta_prep_stages(ins, out)
        for (d, hl, ch), (qp, o0, y) in zip(chains(), out):
            _, r0 = win_start(w, d)
            qp_ref[slot, ch] = qp
            y_ref[slot, ch] = y
            (of_ref if d == 0 else ob_ref)[pl.ds(r0, GDN_WIN), pl.ds(hl * LANES, LANES)] = o0

    def seq_stages(w, slot):
        states = [st_ref[ch] for _, _, ch in chains()]
        for step in range(cpw):
            for i, (d, hl, ch) in enumerate(chains()):
                _, r0 = win_start(w, d)
                hs = slice(hl * LANES, (hl + 1) * LANES)
                dst = of_ref if d == 0 else ob_ref
                j = step if d == 0 else cpw - 1 - step
                rj = pl.multiple_of(r0 + j * CHUNK, CHUNK)
                yj = y_ref[slot, ch, j * GDN_DK:(j + 1) * GDN_DK, :]
                lhs = jnp.concatenate([yj[:, :GDN_DK], qp_ref[slot, ch, j * CHUNK:(j + 1) * CHUNK, :]], axis=0)
                r = _mm(lhs, states[i].astype(BF16))
                glv = col_ref[pl.ds(rj, 1), :][:, 4 * hg + ch:4 * hg + ch + 1]
                dst[pl.ds(rj, CHUNK), hs] = dst[pl.ds(rj, CHUNK), hs] + r[GDN_DK:]
                states[i] = states[i] * jnp.exp(glv) - r[:GDN_DK] + yj[:, GDN_DK:].astype(F32)
            yield
        for i, (_, _, ch) in enumerate(chains()):
            st_ref[ch] = states[i]

    def run(*gens):
        live = list(gens)
        while live:
            for g in list(live):
                try:
                    next(g)
                except StopIteration:
                    live.remove(g)

    run(prep_stages(0, 0))

    def body(w, carry):
        slot = lax.rem(w, 2)
        run(seq_stages(w, slot), prep_stages(w + 1, 1 - slot))
        return carry

    lax.fori_loop(0, nwin - 1, body, 0)
    run(seq_stages(nwin - 1, (nwin - 1) % 2))

    rows = 256
    for t in range(S // rows):
        rs = slice(t * rows, (t + 1) * rows)
        for hl in range(hg):
            hs = slice(hl * LANES, (hl + 1) * LANES)
            o = of_ref[rs, hs] + ob_ref[rs, hs]
            ms = jnp.mean(o * o, axis=-1, keepdims=True)
            gate = gate_ref[rs, hs].astype(F32)
            y = o * lax.rsqrt(ms + EPS) * ng_ref[...]
            o_ref[rs, hs] = (y * (gate * jax.nn.sigmoid(gate))).astype(o_ref.dtype)


def gdn_scan(qkvc, big, gate_col_block0, col, row, norm_g, *, B, S):
    T = B * S
    hg = GDN_HG
    ngroups = GDN_HEADS // hg
    w = hg * LANES
    nwin = S // GDN_WIN
    cpw = GDN_WIN // CHUNK
    assert S % GDN_WIN == 0 and nwin >= 2
    blocks = [5 * _nbytes((S, w), BF16), _nbytes((S, LANES), F32), _nbytes((nwin, 8, GDN_WIN), F32)]
    scratch = (_nbytes((2 * hg, GDN_DK, GDN_DV), F32) + 2 * _nbytes((S, w), F32)
               + _nbytes((2, 2 * hg, GDN_WIN, GDN_DK), BF16) + _nbytes((2, 2 * hg, cpw * GDN_DK, 2 * GDN_DV), BF16))
    return pl.pallas_call(
        functools.partial(_gdn_kernel, S=S, hg=hg),
        grid=(B, ngroups),
        in_specs=[
            pl.BlockSpec((S, w), lambda b, g: (b, g)),
            pl.BlockSpec((S, w), lambda b, g: (b, ngroups + g)),
            pl.BlockSpec((S, w), lambda b, g: (b, 2 * ngroups + g)),
            pl.BlockSpec((S, w), lambda b, g: (b, gate_col_block0 + g)),
            pl.BlockSpec((None, None, S, 6 * hg), lambda b, g: (b, g, 0, 0)),
            pl.BlockSpec((None, None, nwin, 2 * hg, GDN_WIN), lambda b, g: (b, g, 0, 0, 0)),
            pl.BlockSpec((1, GDN_DV), lambda b, g: (0, 0)),
        ],
        out_specs=pl.BlockSpec((S, w), lambda b, g: (b, g)),
        out_shape=jax.ShapeDtypeStruct((T, GDN_HEADS * GDN_DV), BF16),
        scratch_shapes=[
            pltpu.VMEM((2 * hg, GDN_DK, GDN_DV), F32),
            pltpu.VMEM((S, w), F32),
            pltpu.VMEM((S, w), F32),
            pltpu.VMEM((2, 2 * hg, GDN_WIN, GDN_DK), BF16),
            pltpu.VMEM((2, 2 * hg, cpw * GDN_DK, 2 * GDN_DV), BF16),
        ],
        compiler_params=_params(("parallel", "parallel"), blocks, scratch),
        name="gdn_scan",
    )(qkvc, qkvc, qkvc, big, col, row, norm_g)


def _pack_bf16_pair(a, b):
    hi = lax.bitcast_convert_type(a.astype(BF16).astype(F32), jnp.uint32)
    lo = lax.bitcast_convert_type(b.astype(BF16).astype(F32), jnp.uint32)
    return hi | lax.shift_right_logical(lo, jnp.uint32(16))


def _unpack_bf16_pair(w):
    a = lax.bitcast_convert_type(w & jnp.uint32(0xFFFF0000), F32)
    b = lax.bitcast_convert_type(lax.shift_left(w, jnp.uint32(16)), F32)
    return a, b


def _router_kernel(h_ref, g_ref, rw_ref, rb_ref, n_ref, idx_ref, gate_ref):
    x = h_ref[...]
    ms = jnp.mean(x * x, axis=-1, keepdims=True)
    n = x * lax.rsqrt(ms + EPS) * g_ref[...]
    half = n.shape[1] // 2
    n_ref[...] = _pack_bf16_pair(n[:, :half], n[:, half:])
    logits = jnp.dot(n, rw_ref[...], precision=HIGHEST, preferred_element_type=F32) + rb_ref[...]
    lane = lax.broadcasted_iota(jnp.int32, logits.shape, 1).astype(F32)
    neg = -jnp.inf
    l = jnp.where(lane < N_EXPERTS, logits, neg)
    vals, idxs = [], []
    for _ in range(TOP_K):
        m = jnp.max(l, axis=-1, keepdims=True)
        idx = jnp.min(jnp.where(l == m, lane, float(LANES)), axis=-1, keepdims=True)
        vals.append(m)
        idxs.append(idx)
        l = jnp.where(lane == idx, neg, l)
    es = [jnp.exp(v - vals[0]) for v in vals]
    tot = es[0] + es[1] + es[2] + es[3]
    iout = jnp.zeros(logits.shape, F32)
    gout = jnp.zeros(logits.shape, F32)
    for k in range(TOP_K):
        iout = jnp.where(lane == k, idxs[k], iout)
        gout = jnp.where(lane == k, es[k] / tot, gout)
    idx_ref[...] = iout.astype(jnp.int32)
    gate_ref[...] = gout


def moe_router(h, g, rw, rb, *, tm):
    T, D = h.shape
    blocks = [2 * _nbytes((tm, D), F32), _nbytes((D, LANES), F32), 2 * _nbytes((tm, LANES), F32)]
    return pl.pallas_call(
        _router_kernel,
        grid=(T // tm,),
        in_specs=[
            pl.BlockSpec((tm, D), lambda i: (i, 0)),
            pl.BlockSpec((1, D), lambda i: (0, 0)),
            pl.BlockSpec((D, LANES), lambda i: (0, 0)),
            pl.BlockSpec((1, LANES), lambda i: (0, 0)),
        ],
        out_specs=[
            pl.BlockSpec((tm, D // 2), lambda i: (i, 0)),
            pl.BlockSpec((tm, LANES), lambda i: (i, 0)),
            pl.BlockSpec((tm, LANES), lambda i: (i, 0)),
        ],
        out_shape=[
            jax.ShapeDtypeStruct((T, D // 2), jnp.uint32),
            jax.ShapeDtypeStruct((T, LANES), jnp.int32),
            jax.ShapeDtypeStruct((T, LANES), F32),
        ],
        compiler_params=_params(("parallel",), blocks, 4 * _nbytes((tm, D), F32)),
        name="moe_router",
    )(h, g, rw, rb)


def _row_copy(src_hbm, row, dst_ref, dst_row, sem):
    return pltpu.make_async_copy(src_hbm.at[pl.ds(row, 1), :], dst_ref.at[pl.ds(dst_row, 1), :], sem)


def _dispatch_kernel(tok_ref, nu_ref, src_hbm, o_ref, buf_ref, sem, *, bm):
    i = pl.program_id(0)
    nu = nu_ref[0]

    def start_block(blk):
        slot = lax.rem(blk, 2)

        def issue(r, c):
            _row_copy(src_hbm, tok_ref[blk * bm + r], buf_ref.at[slot], r, sem.at[slot]).start()
            return c

        lax.fori_loop(0, bm, issue, 0, unroll=8)

    @pl.when(i == 0)
    def _():
        start_block(i)

    @pl.when(i + 1 < nu)
    def _():
        start_block(i + 1)

    @pl.when(i < nu)
    def _():
        slot = lax.rem(i, 2)
        pltpu.make_async_copy(src_hbm.at[pl.ds(0, bm), :], buf_ref.at[slot], sem.at[slot]).wait()
        o_ref[...] = buf_ref[slot]

    @pl.when(i >= nu)
    def _():
        o_ref[...] = jnp.zeros(o_ref.shape, o_ref.dtype)


def moe_dispatch(row_tok, n_used, src, *, rows, bm):
    W = src.shape[1]
    blocks = [_nbytes((bm, W), src.dtype)]
    return pl.pallas_call(
        functools.partial(_dispatch_kernel, bm=bm),
        grid_spec=pltpu.PrefetchScalarGridSpec(
            num_scalar_prefetch=2,
            grid=(rows // bm,),
            in_specs=[pl.BlockSpec(memory_space=pl.ANY)],
            out_specs=pl.BlockSpec((bm, W), lambda i, tok, nu: (i, 0)),
            scratch_shapes=[pltpu.VMEM((2, bm, W), src.dtype), pltpu.SemaphoreType.DMA((2,))],
        ),
        out_shape=jax.ShapeDtypeStruct((rows, W), src.dtype),
        compiler_params=_params(("arbitrary",), blocks, _nbytes((2, bm, W), src.dtype)),
        name="moe_dispatch",
    )(row_tok, n_used, src)


CAST_ROWS = 256


def _stream_expert_weights(plan, j, i, nj, copies, convert):
    be_ref, first_ref, next_e_ref, last_ref = plan

    @pl.when(jnp.logical_and(j == 0, i == 0))
    def _():
        for c in copies(be_ref[0], 0):
            c.start()

    @pl.when(first_ref[i] == 1)
    def _():
        for c in copies(be_ref[i], j):
            c.wait()
        convert()
        is_last = last_ref[i]

        @pl.when(jnp.logical_or(is_last == 0, j + 1 < nj))
        def _():
            for c in copies(next_e_ref[i], j + is_last):
                c.start()


def _cast_rows(src_ref, dst_ref, col0, width, rows):
    def body(r, c):
        sl = pl.ds(pl.multiple_of(r * CAST_ROWS, CAST_ROWS), CAST_ROWS)
        dst_ref[sl, col0:col0 + width] = src_ref[sl, :].astype(BF16)
        return c

    lax.fori_loop(0, rows // CAST_ROWS, body, 0)


def _expert_up_kernel(be_ref, first_ref, next_e_ref, last_ref, nu_ref, x_ref, w_hbm, bg_ref, bu_ref, o_ref,
                      stage_ref, wbf_ref, sem, *, tn, nj, dff, D):
    j = pl.program_id(0)
    i = pl.program_id(1)

    def copies(e, jj):
        return [pltpu.make_async_copy(w_hbm.at[e, :, pl.ds(pl.multiple_of(h * dff + jj * tn, tn), tn)],
                                      stage_ref.at[h], sem.at[h]) for h in range(2)]

    def convert():
        for h in range(2):
            _cast_rows(stage_ref.at[h], wbf_ref, h * tn, tn, D)

    _stream_expert_weights((be_ref, first_ref, next_e_ref, last_ref), j, i, nj, copies, convert)

    @pl.when(i < nu_ref[0])
    def _():
        xa, xb = _unpack_bf16_pair(x_ref[...])
        half = D // 2
        gu = _mm(xa.astype(BF16), wbf_ref[:half, :]) + _mm(xb.astype(BF16), wbf_ref[half:, :])
        gate = jnp.minimum(gu[:, :tn] + bg_ref[...], SWIGLU_LIMIT)
        up = jnp.clip(gu[:, tn:] + bu_ref[...], -SWIGLU_LIMIT, SWIGLU_LIMIT)
        act = (up + 1.0) * gate * jax.nn.sigmoid(SWIGLU_ALPHA * gate)
        o_ref[...] = act.astype(o_ref.dtype)

    @pl.when(i >= nu_ref[0])
    def _():
        o_ref[...] = jnp.zeros(o_ref.shape, o_ref.dtype)


def moe_expert_up(plan, n_used, xs, w_gu, b_gu, *, bm, tn):
    R = xs.shape[0]
    D = w_gu.shape[1]
    dff = w_gu.shape[2] // 2
    tn = min(tn, dff)
    nj = dff // tn
    nb = R // bm

    def last(i, nu):
        return jnp.minimum(i, nu[0] - 1)

    blocks = [_nbytes((bm, D), BF16), 2 * _nbytes((8, tn), F32), _nbytes((bm, tn), BF16)]
    scratch = _nbytes((2, D, tn), F32) + _nbytes((D, 2 * tn), BF16) + 4 * _nbytes((bm, tn), F32)
    return pl.pallas_call(
        functools.partial(_expert_up_kernel, tn=tn, nj=nj, dff=dff, D=D),
        grid_spec=pltpu.PrefetchScalarGridSpec(
            num_scalar_prefetch=5,
            grid=(nj, nb),
            in_specs=[
                pl.BlockSpec((bm, D // 2), lambda j, i, be, f, ne, lg, nu: (last(i, nu), 0)),
                pl.BlockSpec(memory_space=pl.ANY),
                pl.BlockSpec((None, 1, tn), lambda j, i, be, f, ne, lg, nu: (be[last(i, nu)], 0, j)),
                pl.BlockSpec((None, 1, tn), lambda j, i, be, f, ne, lg, nu: (be[last(i, nu)], 0, nj + j)),
            ],
            out_specs=pl.BlockSpec((bm, tn), lambda j, i, be, f, ne, lg, nu: (i, j)),
            scratch_shapes=[pltpu.VMEM((2, D, tn), F32), pltpu.VMEM((D, 2 * tn), BF16), pltpu.SemaphoreType.DMA((2,))],
        ),
        out_shape=jax.ShapeDtypeStruct((R, dff), BF16),
        compiler_params=_params(("arbitrary", "arbitrary"), blocks, scratch),
        name="moe_expert_up",
    )(*plan, n_used, xs, w_gu, b_gu, b_gu)


def _expert_down_kernel(be_ref, first_ref, next_e_ref, last_ref, nu_ref, a_ref, w_hbm, b_ref, o_ref,
                        stage_ref, wbf_ref, sem, *, tn, nj, dff):
    j = pl.program_id(0)
    i = pl.program_id(1)

    def copies(e, jj):
        return [pltpu.make_async_copy(w_hbm.at[e, :, pl.ds(pl.multiple_of(jj * tn, tn), tn)], stage_ref, sem.at[0])]

    def convert():
        _cast_rows(stage_ref, wbf_ref, 0, tn, dff)

    _stream_expert_weights((be_ref, first_ref, next_e_ref, last_ref), j, i, nj, copies, convert)

    @pl.when(i < nu_ref[0])
    def _():
        y = _mm(a_ref[...], wbf_ref[...]) + b_ref[...]
        o_ref[...] = _pack_bf16_pair(y[:, :tn // 2], y[:, tn // 2:])

    @pl.when(i >= nu_ref[0])
    def _():
        o_ref[...] = jnp.zeros(o_ref.shape, o_ref.dtype)


def moe_expert_down(plan, n_used, act, w_down, b_down, *, bm, tn):
    R, dff = act.shape
    D = w_down.shape[2]
    tn = min(tn, D)
    nj = D // tn

    def last(i, nu):
        return jnp.minimum(i, nu[0] - 1)

    blocks = [_nbytes((bm, dff), BF16), _nbytes((8, tn), F32), _nbytes((bm, tn), F32)]
    scratch = _nbytes((dff, tn), F32) + _nbytes((dff, tn), BF16) + _nbytes((bm, tn), F32)
    return pl.pallas_call(
        functools.partial(_expert_down_kernel, tn=tn, nj=nj, dff=dff),
        grid_spec=pltpu.PrefetchScalarGridSpec(
            num_scalar_prefetch=5,
            grid=(nj, R // bm),
            in_specs=[
                pl.BlockSpec((bm, dff), lambda j, i, be, f, ne, lg, nu: (last(i, nu), 0)),
                pl.BlockSpec(memory_space=pl.ANY),
                pl.BlockSpec((None, 1, tn), lambda j, i, be, f, ne, lg, nu: (be[last(i, nu)], 0, j)),
            ],
            out_specs=pl.BlockSpec((bm, tn // 2), lambda j, i, be, f, ne, lg, nu: (i, j)),
            scratch_shapes=[pltpu.VMEM((dff, tn), F32), pltpu.VMEM((dff, tn), BF16), pltpu.SemaphoreType.DMA((1,))],
        ),
        out_shape=jax.ShapeDtypeStruct((R, D // 2), jnp.uint32),
        compiler_params=_params(("arbitrary", "arbitrary"), blocks, scratch),
        name="moe_expert_down",
    )(*plan, n_used, act, w_down, b_down)


def _combine_kernel(dest_ref, h_ref, gt_ref, y_hbm, g_ref, o_ref, buf_ref, sem, *, tb, nblocks, pair_tile):
    i = pl.program_id(0)

    def start_block(blk):
        slot = lax.rem(blk, 2)

        def issue(r, c):
            for k in range(TOP_K):
                _row_copy(y_hbm, dest_ref[(blk * tb + r) * TOP_K + k], buf_ref.at[slot, k], r, sem.at[slot]).start()
            return c

        lax.fori_loop(0, tb, issue, 0, unroll=4)

    @pl.when(i == 0)
    def _():
        start_block(i)

    @pl.when(i + 1 < nblocks)
    def _():
        start_block(i + 1)

    slot = lax.rem(i, 2)
    for k in range(TOP_K):
        pltpu.make_async_copy(y_hbm.at[pl.ds(0, tb), :], buf_ref.at[slot, k], sem.at[slot]).wait()

    def body(r, c):
        sl = pl.ds(pl.multiple_of(r * NORM_ROWS, NORM_ROWS), NORM_ROWS)
        gt = gt_ref[sl, :]
        acc_a = acc_b = None
        for k in range(TOP_K):
            a, b = _unpack_bf16_pair(buf_ref[slot, k, sl, :])
            gk = gt[:, k:k + 1]
            acc_a = a * gk if acc_a is None else acc_a + a * gk
            acc_b = b * gk if acc_b is None else acc_b + b * gk
        pieces = []
        for j in range(acc_a.shape[1] // pair_tile):
            cs = slice(j * pair_tile, (j + 1) * pair_tile)
            pieces += [acc_a[:, cs], acc_b[:, cs]]
        hv = h_ref[sl, :] + jnp.concatenate(pieces, axis=1)
        ms = jnp.mean(hv * hv, axis=-1, keepdims=True)
        o_ref[sl, :] = hv * lax.rsqrt(ms + EPS) * g_ref[...]
        return c

    lax.fori_loop(0, tb // NORM_ROWS, body, 0)


def moe_combine(dest, h, top_gate, y, g, *, tb, pair_tile):
    T, D = h.shape
    blocks = [2 * _nbytes((tb, D), F32), _nbytes((tb, LANES), F32)]
    return pl.pallas_call(
        functools.partial(_combine_kernel, tb=tb, nblocks=T // tb, pair_tile=pair_tile),
        grid_spec=pltpu.PrefetchScalarGridSpec(
            num_scalar_prefetch=1,
            grid=(T // tb,),
            in_specs=[
                pl.BlockSpec((tb, D), lambda i, d: (i, 0)),
                pl.BlockSpec((tb, LANES), lambda i, d: (i, 0)),
                pl.BlockSpec(memory_space=pl.ANY),
                pl.BlockSpec((1, D), lambda i, d: (0, 0)),
            ],
            out_specs=pl.BlockSpec((tb, D), lambda i, d: (i, 0)),
            scratch_shapes=[pltpu.VMEM((2, TOP_K, tb, D // 2), jnp.uint32), pltpu.SemaphoreType.DMA((2,))],
        ),
        out_shape=jax.ShapeDtypeStruct((T, D), F32),
        compiler_params=_params(("arbitrary",), blocks, _nbytes((2, TOP_K, tb, D), F32)),
        name="moe_combine",
    )(dest, h, top_gate, y, g)


def _routing_tables(top_idx, *, bm):
    T = top_idx.shape[0]
    tk = T * TOP_K
    flat_e = top_idx.reshape(tk)
    onehot = (flat_e[:, None] == jnp.arange(N_EXPERTS, dtype=jnp.int32)[None, :]).astype(jnp.int32)
    csum = jnp.cumsum(onehot, axis=0)
    rank = jnp.take_along_axis(csum, flat_e[:, None], axis=1)[:, 0] - 1
    counts = csum[-1]
    padded = (counts + bm - 1) // bm * bm
    pad_end = jnp.cumsum(padded)
    pad_start = pad_end - padded
    dest = (pad_start[flat_e] + rank).astype(jnp.int32)
    nb = tk // bm + N_EXPERTS
    rows = nb * bm
    flat_tok = jnp.repeat(jnp.arange(T, dtype=jnp.int32), TOP_K)
    row_tok = jnp.zeros((rows,), jnp.int32).at[dest].set(flat_tok)
    block_expert = jnp.minimum(
        jnp.searchsorted(pad_end, jnp.arange(nb, dtype=pad_end.dtype) * bm, side="right"), N_EXPERTS - 1
    ).astype(jnp.int32)
    n_used = (pad_end[-1] // bm).astype(jnp.int32).reshape(1)
    blk = jnp.arange(nb, dtype=jnp.int32)
    used = blk < n_used[0]
    be_used = jnp.where(used, block_expert, N_EXPERTS)
    prev = jnp.concatenate([jnp.full((1,), -1, jnp.int32), be_used[:-1]])
    first = jnp.logical_and(used, be_used != prev).astype(jnp.int32)
    nxt = jnp.searchsorted(be_used, be_used, side="right").astype(jnp.int32)
    is_last = (nxt >= n_used[0]).astype(jnp.int32)
    next_e = jnp.where(is_last == 1, block_expert[0], block_expert[jnp.minimum(nxt, nb - 1)]).astype(jnp.int32)
    plan = (block_expert, first, next_e, is_last)
    return dest, row_tok, plan, n_used, rows


def _rot_half(w):
    half = QK_ROPE // 2
    return jnp.concatenate([-w[..., half:], w[..., :half]], axis=-1)


def _layer(h, pos, invf, ln1_g, w_in, q_norm_g, w_q_up, kv_norm_g, w_kv_up, conv_w, a_log, dt_bias, gdn_norm_g,
           w_mla_o, w_gdn_o, w_out, ln2_g, router_w, router_b, w_gu, b_gu, w_down, b_down, out_g, *, B, S):
    T, D = h.shape
    H = MLA_HEADS
    gqkv = 2 * GDN_HEADS * GDN_DK + GDN_HEADS * GDN_DV
    gw = GDN_HEADS * GDN_DV
    o0 = Q_LORA + KV_LORA
    c_kr, c_qkv = o0, o0 + QK_ROPE
    c_gate = c_qkv + gqkv
    c_ab = c_gate + gw
    c_za = c_ab + 4 * GDN_HEADS
    c_zb = c_za + D

    w_kr = w_in[:, c_kr:c_qkv]
    w_kt = _rot_half(w_kr)
    small_cols = [w_in[:, :o0], w_kr, w_kr, w_kt, w_kt, w_in[:, c_ab:c_za]]
    n_small = 2048
    used = o0 + 4 * QK_ROPE + 4 * GDN_HEADS
    small_cols.append(jnp.zeros((D, n_small - used), F32))
    w_small = jnp.concatenate(small_cols, axis=1).astype(BF16)
    w_big = jnp.concatenate([w_in[:, c_qkv:c_gate], w_in[:, c_gate:c_ab], w_in[:, c_za:c_zb], w_in[:, c_zb:]],
                            axis=1).astype(BF16)
    ka_blk = o0 // LANES
    kb_blk = ka_blk + 1
    ab_blk = ka_blk + 2

    wq = w_q_up.reshape(Q_LORA, H, QK_NOPE + QK_ROPE)
    wq_rope = wq[:, :, QK_NOPE:]
    wq2 = jnp.concatenate([wq[:, :, :QK_NOPE], wq_rope, _rot_half(wq_rope)], axis=-1).reshape(Q_LORA, H * 2 * LANES)
    wq2 = wq2.astype(BF16)
    wkv = w_kv_up.reshape(KV_LORA, H, QK_NOPE + V_HEAD)
    wkv2 = jnp.concatenate([wkv[:, :, :QK_NOPE].reshape(KV_LORA, H * QK_NOPE),
                            wkv[:, :, QK_NOPE:].reshape(KV_LORA, H * V_HEAD)], axis=1).astype(BF16)

    small = norm_matmul(h, ln1_g, w_small, out_dtype=F32, tm=512, tn=1024, name="in_proj_small")
    big = norm_matmul(h, ln1_g, w_big, out_dtype=BF16, tm=512, tn=1024, name="in_proj_big")

    q2 = norm_matmul(small, q_norm_g, wq2, out_dtype=BF16, tm=1024, tn=1024, x_col_block=0, name="q_up")
    kv = norm_matmul(small, kv_norm_g, wkv2, out_dtype=BF16, tm=1024, tn=1024, x_col_block=Q_LORA // KV_LORA,
                     name="kv_up")
    score_scale = (QK_NOPE + QK_ROPE) ** -0.5 * LOG2_E
    kr2, cs = rope_prep(pos, invf, small, ka_blk, kb_blk, tm=1024, scale=score_scale)
    o_a = mla_attention(q2, cs, kv, kr2, B=B, S=S, tq=256, scale=score_scale)

    qkvc = gdn_conv(big, conv_w, B=B, S=S)
    pad = jnp.zeros((LANES - 2 * GDN_HEADS,), F32)
    alog_l = jnp.concatenate([a_log.reshape(-1), pad]).reshape(1, LANES)
    dtb_l = jnp.concatenate([dt_bias.reshape(-1), pad]).reshape(1, LANES)
    gp = gdn_gates(small, ab_blk, alog_l, dtb_l, tm=512)
    hg = GDN_HG
    ng = GDN_HEADS // hg
    nch = S // CHUNK
    nwin = S // GDN_WIN
    gp4 = gp[:, :4 * GDN_HEADS].reshape(B, S, 4, ng, hg)
    gcum = gp4[:, :, :2].reshape(B, nch, CHUNK, 2, ng, hg)
    g_end = jnp.stack([gcum[:, :, CHUNK - 1, 0], gcum[:, :, 0, 1]], axis=2)
    g_end = jnp.broadcast_to(g_end[:, :, None], (B, nch, CHUNK, 2, ng, hg)).reshape(B, S, 2, ng, hg)
    col = jnp.concatenate([gp4, g_end], axis=2).transpose(0, 3, 1, 2, 4).reshape(B, ng, S, 6 * hg)
    row = gp4[:, :, :2].reshape(B, nwin, GDN_WIN, 2, ng, hg).transpose(0, 4, 1, 3, 5, 2)
    row = row.reshape(B, ng, nwin, 2 * hg, GDN_WIN)
    o_b = gdn_scan(qkvc, big, gqkv // (hg * LANES), col, row, gdn_norm_g.reshape(1, GDN_DV), B=B, S=S)

    merged = gated_pair_matmul(o_a, w_mla_o.astype(BF16), o_b, w_gdn_o.astype(BF16), big,
                               (gqkv + gw) // 1024, (gqkv + gw + D) // 1024, tm=512, tn=1024, name="mixer_out")
    h1 = residual_matmul(merged, w_out.astype(BF16), h, tm=512, tn=1024, name="out_proj")

    rw = jnp.concatenate([router_w, jnp.zeros((D, LANES - N_EXPERTS), F32)], axis=1)
    rb = jnp.concatenate([router_b, jnp.zeros((LANES - N_EXPERTS,), F32)]).reshape(1, LANES)
    n2, top_idx, top_gate = moe_router(h1, ln2_g, rw, rb, tm=256)
    dest, row_tok, plan, n_used, rows = _routing_tables(top_idx[:, :TOP_K], bm=MOE_BM)
    xs = moe_dispatch(row_tok, n_used, n2, rows=rows, bm=MOE_BM)
    dff = w_gu.shape[2] // 2
    act = moe_expert_up(plan, n_used, xs, w_gu, b_gu.reshape(N_EXPERTS, 1, 2 * dff), bm=MOE_BM, tn=512)
    down_tn = min(2048, D)
    y = moe_expert_down(plan, n_used, act, w_down, b_down.reshape(N_EXPERTS, 1, D), bm=MOE_BM, tn=down_tn)
    return moe_combine(dest, h1, top_gate, y, out_g, tb=128, pair_tile=down_tn // 2)


def kernel(x, positions, ln1_g, w_in, q_norm_g, w_q_up, kv_norm_g, w_kv_up, gdn_conv_w, gdn_a_log, gdn_dt_bias,
           gdn_norm_g, w_mla_o, w_gdn_o, w_out, ln2_g, router_w, router_b, w_gu, b_gu, w_down, b_down, ln_f_g):
    B, S, D = x.shape
    depth = ln1_g.shape[0]
    assert depth == 1, "the final norm is fused into the last layer's expert combine"
    half = QK_ROPE // 2
    inv_freq = ROPE_THETA ** (-jnp.arange(0, QK_ROPE, 2, dtype=F32) / QK_ROPE)
    invf = jnp.tile(inv_freq, LANES // half).reshape(1, LANES)
    pos = positions.reshape(B * S, 1).astype(jnp.int32)
    h = x.reshape(B * S, D)
    l = 0
    out = _layer(h, pos, invf, ln1_g[l].reshape(1, D), w_in[l], q_norm_g[l].reshape(1, Q_LORA), w_q_up[l],
                 kv_norm_g[l].reshape(1, KV_LORA), w_kv_up[l], gdn_conv_w[l], gdn_a_log[l], gdn_dt_bias[l],
                 gdn_norm_g[l], w_mla_o[l], w_gdn_o[l], w_out[l], ln2_g[l].reshape(1, D), router_w[l], router_b[l],
                 w_gu[l], b_gu[l], w_down[l], b_down[l], ln_f_g.reshape(1, D), B=B, S=S)
    return out.reshape(B, S, D)
```

```python
import functools

import jax
import jax.numpy as jnp
from jax import lax
from jax.experimental import pallas as pl
from jax.experimental.pallas import tpu as pltpu

F32 = jnp.float32
BF16 = jnp.bfloat16

EPS = 1e-6
MLA_HEADS = 16
Q_LORA = 1024
KV_LORA = 512
QK_NOPE = 128
QK_ROPE = 64
V_HEAD = 128
ROPE_THETA = 10000.0
GDN_HEADS = 16
GDN_DK = 128
GDN_DV = 128
CONV_W = 5
CHUNK = 64
N_EXPERTS = 32
TOP_K = 4
SWIGLU_LIMIT = 7.0
SWIGLU_ALPHA = 1.702

LANES = 128
V7X_VMEM_BYTES = 64 * 1024 * 1024
VMEM_CAP = V7X_VMEM_BYTES - 8 * 1024 * 1024
LOG2_E = 1.4426950408889634
ATTN_HEADS_PER_STEP = 2
NORM_ROWS = 32
GDN_HG = 2
GDN_WIN = 256
MOE_BM = 512
HIGHEST = lax.Precision.HIGHEST


def _nbytes(shape, dtype):
    n = 1
    for s in shape:
        n *= s
    return n * jnp.dtype(dtype).itemsize


def _params(sem, blocks, scratch=0):
    need = 2 * sum(blocks) + scratch + 16 * 1024 * 1024
    return pltpu.CompilerParams(dimension_semantics=sem, vmem_limit_bytes=int(min(need, VMEM_CAP)))


def _mm(a, b):
    return jnp.dot(a, b, preferred_element_type=F32)


def _mm_nt(a, b):
    return lax.dot_general(a, b, (((1,), (1,)), ((), ())), preferred_element_type=F32)


def _mm_tn(a, b):
    return lax.dot_general(a, b, (((0,), (0,)), ((), ())), preferred_element_type=F32)


def _norm_rows_into(x_ref, g_ref, xn_ref, rows):
    def body(r, c):
        sl = pl.ds(pl.multiple_of(r * NORM_ROWS, NORM_ROWS), NORM_ROWS)
        xv = x_ref[sl, :].astype(F32)
        ms = jnp.mean(xv * xv, axis=-1, keepdims=True)
        xn_ref[sl, :] = (xv * lax.rsqrt(ms + EPS) * g_ref[...]).astype(BF16)
        return c

    lax.fori_loop(0, rows // NORM_ROWS, body, 0)


def _norm_matmul_kernel(x_ref, g_ref, w_ref, o_ref, xn_ref, *, tm):
    @pl.when(pl.program_id(1) == 0)
    def _():
        _norm_rows_into(x_ref, g_ref, xn_ref, tm)

    o_ref[...] = _mm(xn_ref[...], w_ref[...]).astype(o_ref.dtype)


def norm_matmul(x, g, w, *, out_dtype, tm, tn, x_col_block=0, name):
    T = x.shape[0]
    K, N = w.shape
    tm = min(tm, T)
    assert T % tm == 0 and N % tn == 0 and tm % NORM_ROWS == 0
    blocks = [_nbytes((tm, K), x.dtype), _nbytes((K, tn), BF16), _nbytes((tm, tn), out_dtype)]
    return pl.pallas_call(
        functools.partial(_norm_matmul_kernel, tm=tm),
        grid=(T // tm, N // tn),
        in_specs=[
            pl.BlockSpec((tm, K), lambda i, j: (i, x_col_block)),
            pl.BlockSpec((1, K), lambda i, j: (0, 0)),
            pl.BlockSpec((K, tn), lambda i, j: (0, j)),
        ],
        out_specs=pl.BlockSpec((tm, tn), lambda i, j: (i, j)),
        out_shape=jax.ShapeDtypeStruct((T, N), out_dtype),
        scratch_shapes=[pltpu.VMEM((tm, K), BF16)],
        compiler_params=_params(("parallel", "arbitrary"), blocks, _nbytes((tm, K), BF16)),
        name=name,
    )(x, g, w)


def _gated_pair_kernel(a_ref, wa_ref, za_ref, b_ref, wb_ref, zb_ref, o_ref):
    ya = _mm(a_ref[...], wa_ref[...])
    yb = _mm(b_ref[...], wb_ref[...])
    ga = jax.nn.sigmoid(za_ref[...].astype(F32))
    gb = jax.nn.sigmoid(zb_ref[...].astype(F32))
    o_ref[...] = (ga * ya + gb * yb).astype(o_ref.dtype)


def gated_pair_matmul(a, wa, b, wb, z, za_col_block, zb_col_block, *, tm, tn, name):
    T, K = a.shape
    N = wa.shape[1]
    blocks = [2 * _nbytes((tm, K), BF16), 2 * _nbytes((K, tn), BF16), 3 * _nbytes((tm, tn), BF16)]
    return pl.pallas_call(
        _gated_pair_kernel,
        grid=(T // tm, N // tn),
        in_specs=[
            pl.BlockSpec((tm, K), lambda i, j: (i, 0)),
            pl.BlockSpec((K, tn), lambda i, j: (0, j)),
            pl.BlockSpec((tm, tn), lambda i, j: (i, za_col_block + j)),
            pl.BlockSpec((tm, K), lambda i, j: (i, 0)),
            pl.BlockSpec((K, tn), lambda i, j: (0, j)),
            pl.BlockSpec((tm, tn), lambda i, j: (i, zb_col_block + j)),
        ],
        out_specs=pl.BlockSpec((tm, tn), lambda i, j: (i, j)),
        out_shape=jax.ShapeDtypeStruct((T, N), BF16),
        compiler_params=_params(("parallel", "parallel"), blocks),
        name=name,
    )(a, wa, z, b, wb, z)


def _residual_matmul_kernel(x_ref, w_ref, r_ref, o_ref):
    o_ref[...] = r_ref[...] + _mm(x_ref[...], w_ref[...])


def residual_matmul(x, w, r, *, tm, tn, name):
    T, K = x.shape
    N = w.shape[1]
    blocks = [_nbytes((tm, K), BF16), _nbytes((K, tn), BF16), 2 * _nbytes((tm, tn), F32)]
    return pl.pallas_call(
        _residual_matmul_kernel,
        grid=(T // tm, N // tn),
        in_specs=[
            pl.BlockSpec((tm, K), lambda i, j: (i, 0)),
            pl.BlockSpec((K, tn), lambda i, j: (0, j)),
            pl.BlockSpec((tm, tn), lambda i, j: (i, j)),
        ],
        out_specs=pl.BlockSpec((tm, tn), lambda i, j: (i, j)),
        out_shape=jax.ShapeDtypeStruct((T, N), F32),
        compiler_params=_params(("parallel", "parallel"), blocks),
        name=name,
    )(x, w, r)


def _rope_prep_kernel(pos_ref, invf_ref, ka_ref, kb_ref, kr_ref, cs_ref, *, scale):
    ang = pos_ref[...].astype(F32) * invf_ref[...]
    c = jnp.cos(ang)
    s = jnp.sin(ang)
    kr_ref[...] = (ka_ref[...] * c + kb_ref[...] * s).astype(kr_ref.dtype)
    lane = lax.broadcasted_iota(jnp.int32, c.shape, 1)
    cs_ref[...] = jnp.where(lane < QK_ROPE, c, s) * scale


def rope_prep(pos, invf, small, ka_col_block, kb_col_block, *, tm, scale):
    T = pos.shape[0]
    blocks = [4 * _nbytes((tm, LANES), F32), _nbytes((tm, LANES), F32)]
    return pl.pallas_call(
        functools.partial(_rope_prep_kernel, scale=scale),
        grid=(T // tm,),
        in_specs=[
            pl.BlockSpec((tm, 1), lambda i: (i, 0)),
            pl.BlockSpec((1, LANES), lambda i: (0, 0)),
            pl.BlockSpec((tm, LANES), lambda i: (i, ka_col_block)),
            pl.BlockSpec((tm, LANES), lambda i: (i, kb_col_block)),
        ],
        out_specs=[pl.BlockSpec((tm, LANES), lambda i: (i, 0)), pl.BlockSpec((tm, LANES), lambda i: (i, 0))],
        out_shape=[jax.ShapeDtypeStruct((T, LANES), BF16), jax.ShapeDtypeStruct((T, LANES), F32)],
        compiler_params=_params(("parallel",), blocks),
        name="rope_prep",
    )(pos, invf, small, small)


def _attention_kernel(q_ref, cs_ref, kn_ref, kr_ref, v_ref, o_ref, kcat_ref, *, scale, nh):
    @pl.when(pl.program_id(2) == 0)
    def _():
        for a in range(nh):
            kcat_ref[a, :, :QK_NOPE] = kn_ref[:, a * QK_NOPE:(a + 1) * QK_NOPE]
            kcat_ref[a, :, QK_NOPE:] = kr_ref[...]

    heads = range(nh)
    qq = []
    for a in heads:
        q = q_ref[:, a * 2 * LANES:(a + 1) * 2 * LANES]
        qn = (q[:, :QK_NOPE].astype(F32) * scale).astype(BF16)
        qr = (q[:, QK_NOPE:].astype(F32) * cs_ref[...]).astype(BF16)
        qq.append(jnp.concatenate([qn, qr], axis=1))
    s = [_mm_nt(qq[a], kcat_ref[a]) for a in heads]
    m = [jnp.max(s[a], axis=-1, keepdims=True) for a in heads]
    p = [jnp.exp2(s[a] - m[a]) for a in heads]
    l = [jnp.sum(p[a], axis=-1, keepdims=True) for a in heads]
    o = [_mm(p[a].astype(BF16), v_ref[:, a * V_HEAD:(a + 1) * V_HEAD]) for a in heads]
    for a in heads:
        o_ref[:, a * V_HEAD:(a + 1) * V_HEAD] = (o[a] / l[a]).astype(o_ref.dtype)


def mla_attention(q2, cs, kv, kr2, *, B, S, tq, scale):
    T = B * S
    H = MLA_HEADS
    nh = ATTN_HEADS_PER_STEP
    nq = S // tq
    wq, wk = nh * 2 * LANES, nh * LANES
    blocks = [_nbytes((tq, wq), BF16), _nbytes((tq, LANES), F32), 2 * _nbytes((S, wk), BF16),
              _nbytes((S, LANES), BF16), _nbytes((tq, wk), BF16)]
    scratch = _nbytes((nh, S, 2 * LANES), BF16) + 3 * nh * _nbytes((tq, S), F32)
    return pl.pallas_call(
        functools.partial(_attention_kernel, scale=scale, nh=nh),
        grid=(B, H // nh, nq),
        in_specs=[
            pl.BlockSpec((tq, wq), lambda b, h, i: (b * nq + i, h)),
            pl.BlockSpec((tq, LANES), lambda b, h, i: (b * nq + i, 0)),
            pl.BlockSpec((S, wk), lambda b, h, i: (b, h)),
            pl.BlockSpec((S, LANES), lambda b, h, i: (b, 0)),
            pl.BlockSpec((S, wk), lambda b, h, i: (b, H // nh + h)),
        ],
        out_specs=pl.BlockSpec((tq, wk), lambda b, h, i: (b * nq + i, h)),
        out_shape=jax.ShapeDtypeStruct((T, H * V_HEAD), BF16),
        scratch_shapes=[pltpu.VMEM((nh, S, 2 * LANES), BF16)],
        compiler_params=_params(("parallel", "parallel", "arbitrary"), blocks, scratch),
        name="mla_attention",
    )(q2, cs, kv, kr2, kv)


def _conv_kernel(x_ref, w_ref, o_ref, xp_ref, *, S, rows, n_qk_blocks, n_q_blocks):
    pad = 8
    cb = pl.program_id(1)
    xp_ref[0:pad, :] = jnp.zeros((pad, LANES), F32)
    xp_ref[S + pad:S + 2 * pad, :] = jnp.zeros((pad, LANES), F32)
    for t in range(S // rows):
        xp_ref[pad + t * rows:pad + (t + 1) * rows, :] = x_ref[t * rows:(t + 1) * rows, :].astype(F32)
    w = w_ref[...]
    half = CONV_W // 2
    q_scale = jnp.where(cb < n_q_blocks, GDN_DK ** -0.5, 1.0).astype(F32)
    for t in range(S // rows):
        acc = jnp.zeros((rows, LANES), F32)
        for k in range(CONV_W):
            off = pad + t * rows + k - half
            acc = acc + xp_ref[off:off + rows, :] * w[k:k + 1, :]
        y = acc * jax.nn.sigmoid(acc)
        ss = jnp.sum(y * y, axis=-1, keepdims=True)
        f = jnp.where(cb < n_qk_blocks, lax.rsqrt(ss + EPS) * q_scale, 1.0)
        o_ref[t * rows:(t + 1) * rows, :] = (y * f).astype(o_ref.dtype)


def gdn_conv(big, conv_w, *, B, S):
    T = B * S
    C = conv_w.shape[1]
    ncb = C // LANES
    n_q = GDN_HEADS * GDN_DK // LANES
    blocks = [2 * _nbytes((S, LANES), BF16), _nbytes((8, LANES), F32)]
    return pl.pallas_call(
        functools.partial(_conv_kernel, S=S, rows=256, n_qk_blocks=2 * n_q, n_q_blocks=n_q),
        grid=(B, ncb),
        in_specs=[
            pl.BlockSpec((S, LANES), lambda b, c: (b, c)),
            pl.BlockSpec((CONV_W, LANES), lambda b, c: (0, c)),
        ],
        out_specs=pl.BlockSpec((S, LANES), lambda b, c: (b, c)),
        out_shape=jax.ShapeDtypeStruct((T, C), BF16),
        scratch_shapes=[pltpu.VMEM((S + 16, LANES), F32)],
        compiler_params=_params(("parallel", "parallel"), blocks, _nbytes((S + 16, LANES), F32)),
        name="gdn_conv",
    )(big, conv_w)


def _gates_kernel(ab_ref, alog_ref, dtb_ref, o_ref, *, tm):
    H = GDN_HEADS
    x = ab_ref[...]
    z = x + dtb_ref[...]
    softplus = jnp.maximum(z, 0.0) + jnp.log(1.0 + jnp.exp(-jnp.abs(z)))
    g = -jnp.exp(alog_ref[...]) * softplus
    beta = jax.nn.sigmoid(x)
    lane = lax.broadcasted_iota(jnp.int32, (CHUNK, LANES), 1)
    ri = lax.broadcasted_iota(jnp.int32, (CHUNK, CHUNK), 0)
    ci = lax.broadcasted_iota(jnp.int32, (CHUNK, CHUNK), 1)
    tril = (ri >= ci).astype(F32)
    triu = (ri <= ci).astype(F32)
    for t in range(tm // CHUNK):
        sl = slice(t * CHUNK, (t + 1) * CHUNK)
        gc = g[sl]
        pre = jnp.dot(tril, gc, precision=HIGHEST, preferred_element_type=F32)
        suf = jnp.dot(triu, gc, precision=HIGHEST, preferred_element_type=F32)
        cum = jnp.where(lane < H, pre, suf)
        o_ref[sl, :] = jnp.where(lane < 2 * H, cum, beta[sl])


def gdn_gates(small, ab_col_block, alog_l, dtb_l, *, tm):
    T = small.shape[0]
    blocks = [2 * _nbytes((tm, LANES), F32)]
    return pl.pallas_call(
        functools.partial(_gates_kernel, tm=tm),
        grid=(T // tm,),
        in_specs=[
            pl.BlockSpec((tm, LANES), lambda i: (i, ab_col_block)),
            pl.BlockSpec((1, LANES), lambda i: (0, 0)),
            pl.BlockSpec((1, LANES), lambda i: (0, 0)),
        ],
        out_specs=pl.BlockSpec((tm, LANES), lambda i: (i, 0)),
        out_shape=jax.ShapeDtypeStruct((T, LANES), F32),
        compiler_params=_params(("parallel",), blocks),
        name="gdn_gates",
    )(small, alog_l, dtb_l)


def _window_masks(reverse):
    w = GDN_WIN
    ri = lax.broadcasted_iota(jnp.int32, (w, w), 0)
    ci = lax.broadcasted_iota(jnp.int32, (w, w), 1)
    same = jnp.right_shift(ri, 6) == jnp.right_shift(ci, 6)
    order_incl = (ri <= ci) if reverse else (ri >= ci)
    order_strict = (ri < ci) if reverse else (ri > ci)
    return dict(
        incl=jnp.logical_and(same, order_incl),
        strict=jnp.logical_and(same, order_strict),
        b16=jnp.right_shift(ri, 4) == jnp.right_shift(ci, 4),
        b32=jnp.right_shift(ri, 5) == jnp.right_shift(ci, 5),
        diag=ri == ci,
    )


def _delta_prep_stages(chains, out):
    w = GDN_WIN

    def bf(x):
        return x.astype(BF16)

    def cat(x, y):
        return jnp.concatenate([bf(x), bf(y)], axis=1)

    n = range(len(chains))
    gram = [_mm_nt(jnp.concatenate([c["q"], c["k"]], axis=0), c["k"]) for c in chains]
    yield
    decay = [jnp.exp(jnp.where(c["m"]["incl"], c["gc"] - c["gr"], -jnp.inf)) for c in chains]
    a = [jnp.where(chains[i]["m"]["strict"], gram[i][w:] * chains[i]["bc"] * decay[i], 0.0) for i in n]
    nd = [jnp.where(chains[i]["m"]["b16"], -a[i], 0.0) for i in n]
    n2 = [_mm(bf(nd[i]), bf(nd[i])) for i in n]
    yield
    p = [jnp.where(chains[i]["m"]["diag"], 1.0, nd[i]) for i in n]
    r = [_mm(bf(n2[i]), cat(p[i], n2[i])) for i in n]
    yield
    p = [p[i] + r[i][:, :w] for i in n]
    n4 = [r[i][:, w:] for i in n]
    r = [_mm(bf(n4[i]), cat(p[i], n4[i])) for i in n]
    yield
    p = [p[i] + r[i][:, :w] for i in n]
    n8 = [r[i][:, w:] for i in n]
    r = [_mm(bf(n8[i]), bf(p[i])) for i in n]
    yield
    di = [p[i] + r[i] for i in n]
    lo = [jnp.where(chains[i]["m"]["b16"], 0.0, a[i]) for i in n]
    nt = [-_mm(bf(di[i]), bf(lo[i])) for i in n]
    yield
    r = [_mm(bf(nt[i]), cat(nt[i], di[i])) for i in n]
    yield
    s1 = [di[i] + r[i][:, w:] for i in n]
    t = [s1[i] + _mm(bf(r[i][:, :w]), bf(s1[i])) for i in n]
    yield
    eg = [jnp.exp(c["gc"]) for c in chains]
    kf = [c["k"].astype(F32) for c in chains]
    rhs = [cat(kf[i] * (chains[i]["bc"] * eg[i]), chains[i]["v"].astype(F32) * chains[i]["bc"]) for i in n]
    wu = [bf(_mm(bf(t[i]), rhs[i])) for i in n]
    yield
    col_chunk = jnp.right_shift(lax.broadcasted_iota(jnp.int32, (GDN_DK, w), 1), 6)
    lhs = []
    for i in n:
        c = chains[i]
        qk = jnp.where(c["m"]["incl"], gram[i][:w] * decay[i], 0.0)
        k_end_t = jnp.transpose(kf[i] * jnp.exp(c["glc"] - c["gc"]))
        spread = [jnp.where(col_chunk == j, k_end_t, 0.0) for j in range(w // CHUNK)]
        lhs.append(bf(jnp.concatenate([qk] + spread, axis=0)))
    res = [_mm(lhs[i], wu[i]) for i in n]
    yield
    for i in n:
        c = chains[i]
        x = res[i][:w]
        qp = c["q"].astype(F32) * eg[i] - x[:, :GDN_DK]
        out.append((bf(qp), x[:, GDN_DK:], bf(res[i][w:])))


def _gdn_kernel(q_ref, k_ref, v_ref, gate_ref, col_ref, row_ref, ng_ref, o_ref,
                st_ref, of_ref, ob_ref, qp_ref, y_ref, *, S, hg):
    nwin = S // GDN_WIN
    cpw = GDN_WIN // CHUNK
    st_ref[...] = jnp.zeros(st_ref.shape, F32)

    def chains():
        for d in range(2):
            for hl in range(hg):
                yield d, hl, d * hg + hl

    def win_start(w, d):
        wi = w if d == 0 else nwin - 1 - w
        return wi, pl.multiple_of(wi * GDN_WIN, GDN_WIN)

    def prep_stages(w, slot):
        masks = (_window_masks(False), _window_masks(True))
        ins = []
        for d, hl, ch in chains():
            wi, r0 = win_start(w, d)
            hs = slice(hl * LANES, (hl + 1) * LANES)
            colv = col_ref[pl.ds(r0, GDN_WIN), :]
            rowv = row_ref[wi]
            gi, bi, li = ch, 2 * hg + ch, 4 * hg + ch
            ins.append(dict(q=q_ref[pl.ds(r0, GDN_WIN), hs], k=k_ref[pl.ds(r0, GDN_WIN), hs],
                            v=v_ref[pl.ds(r0, GDN_WIN), hs], gc=colv[:, gi:gi + 1], gr=rowv[gi:gi + 1, :],
                            bc=colv[:, bi:bi + 1], glc=colv[:, li:li + 1], m=masks[d]))
        out = []
        yield from _delta_prep_stages(ins, out)
        for (d, hl, ch), (qp, o0, y) in zip(chains(), out):
            _, r0 = win_start(w, d)
            qp_ref[slot, ch] = qp
            y_ref[slot, ch] = y
            (of_ref if d == 0 else ob_ref)[pl.ds(r0, GDN_WIN), pl.ds(hl * LANES, LANES)] = o0

    def seq_stages(w, slot):
        states = [st_ref[ch] for _, _, ch in chains()]
        for step in range(cpw):
            for i, (d, hl, ch) in enumerate(chains()):
                _, r0 = win_start(w, d)
                hs = slice(hl * LANES, (hl + 1) * LANES)
                dst = of_ref if d == 0 else ob_ref
                j = step if d == 0 else cpw - 1 - step
                rj = pl.multiple_of(r0 + j * CHUNK, CHUNK)
                yj = y_ref[slot, ch, j * GDN_DK:(j + 1) * GDN_DK, :]
                lhs = jnp.concatenate([yj[:, :GDN_DK], qp_ref[slot, ch, j * CHUNK:(j + 1) * CHUNK, :]], axis=0)
                r = _mm(lhs, states[i].astype(BF16))
                glv = col_ref[pl.ds(rj, 1), :][:, 4 * hg + ch:4 * hg + ch + 1]
                dst[pl.ds(rj, CHUNK), hs] = dst[pl.ds(rj, CHUNK), hs] + r[GDN_DK:]
                states[i] = states[i] * jnp.exp(glv) - r[:GDN_DK] + yj[:, GDN_DK:].astype(F32)
            yield
        for i, (_, _, ch) in enumerate(chains()):
            st_ref[ch] = states[i]

    def run(*gens):
        live = list(gens)
        while live:
            for g in list(live):
                try:
                    next(g)
                except StopIteration:
                    live.remove(g)

    run(prep_stages(0, 0))

    def body(w, carry):
        slot = lax.rem(w, 2)
        run(seq_stages(w, slot), prep_stages(w + 1, 1 - slot))
        return carry

    lax.fori_loop(0, nwin - 1, body, 0)
    run(seq_stages(nwin - 1, (nwin - 1) % 2))

    rows = 256
    for t in range(S // rows):
        rs = slice(t * rows, (t + 1) * rows)
        for hl in range(hg):
            hs = slice(hl * LANES, (hl + 1) * LANES)
            o = of_ref[rs, hs] + ob_ref[rs, hs]
            ms = jnp.mean(o * o, axis=-1, keepdims=True)
            gate = gate_ref[rs, hs].astype(F32)
            y = o * lax.rsqrt(ms + EPS) * ng_ref[...]
            o_ref[rs, hs] = (y * (gate * jax.nn.sigmoid(gate))).astype(o_ref.dtype)


def gdn_scan(qkvc, big, gate_col_block0, col, row, norm_g, *, B, S):
    T = B * S
    hg = GDN_HG
    ngroups = GDN_HEADS // hg
    w = hg * LANES
    nwin = S // GDN_WIN
    cpw = GDN_WIN // CHUNK
    assert S % GDN_WIN == 0 and nwin >= 2
    blocks = [5 * _nbytes((S, w), BF16), _nbytes((S, LANES), F32), _nbytes((nwin, 8, GDN_WIN), F32)]
    scratch = (_nbytes((2 * hg, GDN_DK, GDN_DV), F32) + 2 * _nbytes((S, w), F32)
               + _nbytes((2, 2 * hg, GDN_WIN, GDN_DK), BF16) + _nbytes((2, 2 * hg, cpw * GDN_DK, 2 * GDN_DV), BF16))
    return pl.pallas_call(
        functools.partial(_gdn_kernel, S=S, hg=hg),
        grid=(B, ngroups),
        in_specs=[
            pl.BlockSpec((S, w), lambda b, g: (b, g)),
            pl.BlockSpec((S, w), lambda b, g: (b, ngroups + g)),
            pl.BlockSpec((S, w), lambda b, g: (b, 2 * ngroups + g)),
            pl.BlockSpec((S, w), lambda b, g: (b, gate_col_block0 + g)),
            pl.BlockSpec((None, None, S, 6 * hg), lambda b, g: (b, g, 0, 0)),
            pl.BlockSpec((None, None, nwin, 2 * hg, GDN_WIN), lambda b, g: (b, g, 0, 0, 0)),
            pl.BlockSpec((1, GDN_DV), lambda b, g: (0, 0)),
        ],
        out_specs=pl.BlockSpec((S, w), lambda b, g: (b, g)),
        out_shape=jax.ShapeDtypeStruct((T, GDN_HEADS * GDN_DV), BF16),
        scratch_shapes=[
            pltpu.VMEM((2 * hg, GDN_DK, GDN_DV), F32),
            pltpu.VMEM((S, w), F32),
            pltpu.VMEM((S, w), F32),
            pltpu.VMEM((2, 2 * hg, GDN_WIN, GDN_DK), BF16),
            pltpu.VMEM((2, 2 * hg, cpw * GDN_DK, 2 * GDN_DV), BF16),
        ],
        compiler_params=_params(("parallel", "parallel"), blocks, scratch),
        name="gdn_scan",
    )(qkvc, qkvc, qkvc, big, col, row, norm_g)


def _pack_bf16_pair(a, b):
    hi = lax.bitcast_convert_type(a.astype(BF16).astype(F32), jnp.uint32)
    lo = lax.bitcast_convert_type(b.astype(BF16).astype(F32), jnp.uint32)
    return hi | lax.shift_right_logical(lo, jnp.uint32(16))


def _unpack_bf16_pair(w):
    a = lax.bitcast_convert_type(w & jnp.uint32(0xFFFF0000), F32)
    b = lax.bitcast_convert_type(lax.shift_left(w, jnp.uint32(16)), F32)
    return a, b


def _router_kernel(h_ref, g_ref, rw_ref, rb_ref, n_ref, idx_ref, gate_ref):
    x = h_ref[...]
    ms = jnp.mean(x * x, axis=-1, keepdims=True)
    n = x * lax.rsqrt(ms + EPS) * g_ref[...]
    half = n.shape[1] // 2
    n_ref[...] = _pack_bf16_pair(n[:, :half], n[:, half:])
    logits = jnp.dot(n, rw_ref[...], precision=HIGHEST, preferred_element_type=F32) + rb_ref[...]
    lane = lax.broadcasted_iota(jnp.int32, logits.shape, 1).astype(F32)
    neg = -jnp.inf
    l = jnp.where(lane < N_EXPERTS, logits, neg)
    vals, idxs = [], []
    for _ in range(TOP_K):
        m = jnp.max(l, axis=-1, keepdims=True)
        idx = jnp.min(jnp.where(l == m, lane, float(LANES)), axis=-1, keepdims=True)
        vals.append(m)
        idxs.append(idx)
        l = jnp.where(lane == idx, neg, l)
    es = [jnp.exp(v - vals[0]) for v in vals]
    tot = es[0] + es[1] + es[2] + es[3]
    iout = jnp.zeros(logits.shape, F32)
    gout = jnp.zeros(logits.shape, F32)
    for k in range(TOP_K):
        iout = jnp.where(lane == k, idxs[k], iout)
        gout = jnp.where(lane == k, es[k] / tot, gout)
    idx_ref[...] = iout.astype(jnp.int32)
    gate_ref[...] = gout


def moe_router(h, g, rw, rb, *, tm):
    T, D = h.shape
    blocks = [2 * _nbytes((tm, D), F32), _nbytes((D, LANES), F32), 2 * _nbytes((tm, LANES), F32)]
    return pl.pallas_call(
        _router_kernel,
        grid=(T // tm,),
        in_specs=[
            pl.BlockSpec((tm, D), lambda i: (i, 0)),
            pl.BlockSpec((1, D), lambda i: (0, 0)),
            pl.BlockSpec((D, LANES), lambda i: (0, 0)),
            pl.BlockSpec((1, LANES), lambda i: (0, 0)),
        ],
        out_specs=[
            pl.BlockSpec((tm, D // 2), lambda i: (i, 0)),
            pl.BlockSpec((tm, LANES), lambda i: (i, 0)),
            pl.BlockSpec((tm, LANES), lambda i: (i, 0)),
        ],
        out_shape=[
            jax.ShapeDtypeStruct((T, D // 2), jnp.uint32),
            jax.ShapeDtypeStruct((T, LANES), jnp.int32),
            jax.ShapeDtypeStruct((T, LANES), F32),
        ],
        compiler_params=_params(("parallel",), blocks, 4 * _nbytes((tm, D), F32)),
        name="moe_router",
    )(h, g, rw, rb)


def _row_copy(src_hbm, row, dst_ref, dst_row, sem):
    return pltpu.make_async_copy(src_hbm.at[pl.ds(row, 1), :], dst_ref.at[pl.ds(dst_row, 1), :], sem)


def _rank_kernel(idx_ref, rank_ref, cnt_ref, carry_ref, *, tb):
    @pl.when(pl.program_id(0) == 0)
    def _():
        carry_ref[...] = jnp.zeros(carry_ref.shape, F32)

    idx = idx_ref[...]
    lane = lax.broadcasted_iota(jnp.int32, idx.shape, 1)
    hits = [lane == idx[:, k:k + 1] for k in range(TOP_K)]
    onehot = jnp.zeros(idx.shape, F32)
    for h in hits:
        onehot = onehot + jnp.where(h, 1.0, 0.0)
    ri = lax.broadcasted_iota(jnp.int32, (tb, tb), 0)
    ci = lax.broadcasted_iota(jnp.int32, (tb, tb), 1)
    tril = jnp.where(ri >= ci, 1.0, 0.0).astype(BF16)
    incl = _mm(tril, onehot.astype(BF16))
    carry = carry_ref[0:1, :]
    before = carry + incl - onehot
    out = jnp.zeros(idx.shape, F32)
    for k in range(TOP_K):
        rk = jnp.sum(jnp.where(hits[k], before, 0.0), axis=-1, keepdims=True)
        out = jnp.where(lane == k, rk, out)
    rank_ref[...] = out.astype(jnp.int32)
    total = jnp.broadcast_to(carry + incl[tb - 1:tb, :], carry_ref.shape)
    carry_ref[...] = total
    cnt_ref[...] = total


def moe_rank(top_idx, *, tb):
    T = top_idx.shape[0]
    blocks = [2 * _nbytes((tb, LANES), F32), _nbytes((8, LANES), F32)]
    return pl.pallas_call(
        functools.partial(_rank_kernel, tb=tb),
        grid=(T // tb,),
        in_specs=[pl.BlockSpec((tb, LANES), lambda i: (i, 0))],
        out_specs=[pl.BlockSpec((tb, LANES), lambda i: (i, 0)), pl.BlockSpec((8, LANES), lambda i: (0, 0))],
        out_shape=[jax.ShapeDtypeStruct((T, LANES), jnp.int32), jax.ShapeDtypeStruct((8, LANES), F32)],
        scratch_shapes=[pltpu.VMEM((8, LANES), F32)],
        compiler_params=_params(("arbitrary",), blocks, 4 * _nbytes((tb, tb), F32)),
        name="moe_rank",
    )(top_idx)


def _scatter_rows_kernel(dest_ref, zs_ref, nu_ref, src_hbm, out_hbm, zero_ref, sem, zsem, *, tpb, nsteps, bm, nb):
    i = pl.program_id(0)
    n_copies = tpb * TOP_K

    @pl.when(i == 0)
    def _():
        zero_ref[...] = jnp.zeros(zero_ref.shape, zero_ref.dtype)

        def zero_block(row0):
            return pltpu.make_async_copy(zero_ref, out_hbm.at[pl.ds(pl.multiple_of(row0, bm), bm), :], zsem)

        def start_unused(b, c):
            zero_block(b * bm).start()
            return c

        def wait_unused(b, c):
            zero_block(b * bm).wait()
            return c

        for e in range(N_EXPERTS):
            zero_block(zs_ref[e]).start()
        lax.fori_loop(nu_ref[0], nb, start_unused, 0)
        for e in range(N_EXPERTS):
            zero_block(zs_ref[e]).wait()
        lax.fori_loop(nu_ref[0], nb, wait_unused, 0)

    def block_wait(blk):
        pltpu.make_async_copy(src_hbm.at[pl.ds(0, n_copies), :], out_hbm.at[pl.ds(0, n_copies), :],
                              sem.at[lax.rem(blk, 2)]).wait()

    slot = lax.rem(i, 2)

    def issue(t, c):
        tok = i * tpb + t
        for k in range(TOP_K):
            pltpu.make_async_copy(src_hbm.at[pl.ds(tok, 1), :], out_hbm.at[pl.ds(dest_ref[tok * TOP_K + k], 1), :],
                                  sem.at[slot]).start()
        return c

    lax.fori_loop(0, tpb, issue, 0, unroll=4)

    @pl.when(i > 0)
    def _():
        block_wait(i - 1)

    @pl.when(i == nsteps - 1)
    def _():
        block_wait(i)


def moe_scatter_rows(dest, zero_start, n_used, src, *, rows, bm, tpb):
    T, W = src.shape
    nsteps = T // tpb
    return pl.pallas_call(
        functools.partial(_scatter_rows_kernel, tpb=tpb, nsteps=nsteps, bm=bm, nb=rows // bm),
        grid_spec=pltpu.PrefetchScalarGridSpec(
            num_scalar_prefetch=3,
            grid=(nsteps,),
            in_specs=[pl.BlockSpec(memory_space=pl.ANY)],
            out_specs=pl.BlockSpec(memory_space=pl.ANY),
            scratch_shapes=[pltpu.VMEM((bm, W), src.dtype), pltpu.SemaphoreType.DMA((2,)), pltpu.SemaphoreType.DMA],
        ),
        out_shape=jax.ShapeDtypeStruct((rows, W), src.dtype),
        compiler_params=_params(("arbitrary",), [], _nbytes((bm, W), src.dtype)),
        name="moe_scatter_rows",
    )(dest, zero_start, n_used, src)


CAST_ROWS = 256


def _stream_expert_weights(plan, j, i, nj, copies, convert):
    be_ref, first_ref, next_e_ref, last_ref = plan

    @pl.when(jnp.logical_and(j == 0, i == 0))
    def _():
        for c in copies(be_ref[0], 0):
            c.start()

    @pl.when(first_ref[i] == 1)
    def _():
        for c in copies(be_ref[i], j):
            c.wait()
        convert()
        is_last = last_ref[i]

        @pl.when(jnp.logical_or(is_last == 0, j + 1 < nj))
        def _():
            for c in copies(next_e_ref[i], j + is_last):
                c.start()


def _cast_rows(src_ref, dst_ref, col0, width, rows):
    def body(r, c):
        sl = pl.ds(pl.multiple_of(r * CAST_ROWS, CAST_ROWS), CAST_ROWS)
        dst_ref[sl, col0:col0 + width] = src_ref[sl, :].astype(BF16)
        return c

    lax.fori_loop(0, rows // CAST_ROWS, body, 0)


def _expert_up_kernel(be_ref, first_ref, next_e_ref, last_ref, nu_ref, x_ref, w_hbm, bg_ref, bu_ref, o_ref,
                      stage_ref, wbf_ref, sem, *, tn, nj, dff, D):
    j = pl.program_id(0)
    i = pl.program_id(1)

    def copies(e, jj):
        return [pltpu.make_async_copy(w_hbm.at[e, :, pl.ds(pl.multiple_of(h * dff + jj * tn, tn), tn)],
                                      stage_ref.at[h], sem.at[h]) for h in range(2)]

    def convert():
        for h in range(2):
            _cast_rows(stage_ref.at[h], wbf_ref, h * tn, tn, D)

    _stream_expert_weights((be_ref, first_ref, next_e_ref, last_ref), j, i, nj, copies, convert)

    @pl.when(i < nu_ref[0])
    def _():
        xa, xb = _unpack_bf16_pair(x_ref[...])
        half = D // 2
        gu = _mm(xa.astype(BF16), wbf_ref[:half, :]) + _mm(xb.astype(BF16), wbf_ref[half:, :])
        gate = jnp.minimum(gu[:, :tn] + bg_ref[...], SWIGLU_LIMIT)
        up = jnp.clip(gu[:, tn:] + bu_ref[...], -SWIGLU_LIMIT, SWIGLU_LIMIT)
        act = (up + 1.0) * gate * jax.nn.sigmoid(SWIGLU_ALPHA * gate)
        o_ref[...] = act.astype(o_ref.dtype)

    @pl.when(i >= nu_ref[0])
    def _():
        o_ref[...] = jnp.zeros(o_ref.shape, o_ref.dtype)


def moe_expert_up(plan, n_used, xs, w_gu, b_gu, *, bm, tn):
    R = xs.shape[0]
    D = w_gu.shape[1]
    dff = w_gu.shape[2] // 2
    tn = min(tn, dff)
    nj = dff // tn
    nb = R // bm

    def last(i, nu):
        return jnp.minimum(i, nu[0] - 1)

    blocks = [_nbytes((bm, D), BF16), 2 * _nbytes((8, tn), F32), _nbytes((bm, tn), BF16)]
    scratch = _nbytes((2, D, tn), F32) + _nbytes((D, 2 * tn), BF16) + 4 * _nbytes((bm, tn), F32)
    return pl.pallas_call(
        functools.partial(_expert_up_kernel, tn=tn, nj=nj, dff=dff, D=D),
        grid_spec=pltpu.PrefetchScalarGridSpec(
            num_scalar_prefetch=5,
            grid=(nj, nb),
            in_specs=[
                pl.BlockSpec((bm, D // 2), lambda j, i, be, f, ne, lg, nu: (last(i, nu), 0)),
                pl.BlockSpec(memory_space=pl.ANY),
                pl.BlockSpec((None, 1, tn), lambda j, i, be, f, ne, lg, nu: (be[last(i, nu)], 0, j)),
                pl.BlockSpec((None, 1, tn), lambda j, i, be, f, ne, lg, nu: (be[last(i, nu)], 0, nj + j)),
            ],
            out_specs=pl.BlockSpec((bm, tn), lambda j, i, be, f, ne, lg, nu: (i, j)),
            scratch_shapes=[pltpu.VMEM((2, D, tn), F32), pltpu.VMEM((D, 2 * tn), BF16), pltpu.SemaphoreType.DMA((2,))],
        ),
        out_shape=jax.ShapeDtypeStruct((R, dff), BF16),
        compiler_params=_params(("arbitrary", "arbitrary"), blocks, scratch),
        name="moe_expert_up",
    )(*plan, n_used, xs, w_gu, b_gu, b_gu)


def _expert_down_kernel(be_ref, first_ref, next_e_ref, last_ref, nu_ref, a_ref, w_hbm, b_ref, o_ref,
                        stage_ref, wbf_ref, sem, *, tn, nj, dff):
    j = pl.program_id(0)
    i = pl.program_id(1)

    def copies(e, jj):
        return [pltpu.make_async_copy(w_hbm.at[e, :, pl.ds(pl.multiple_of(jj * tn, tn), tn)], stage_ref, sem.at[0])]

    def convert():
        _cast_rows(stage_ref, wbf_ref, 0, tn, dff)

    _stream_expert_weights((be_ref, first_ref, next_e_ref, last_ref), j, i, nj, copies, convert)

    @pl.when(i < nu_ref[0])
    def _():
        y = _mm(a_ref[...], wbf_ref[...]) + b_ref[...]
        o_ref[...] = _pack_bf16_pair(y[:, :tn // 2], y[:, tn // 2:])

    @pl.when(i >= nu_ref[0])
    def _():
        o_ref[...] = jnp.zeros(o_ref.shape, o_ref.dtype)


def moe_expert_down(plan, n_used, act, w_down, b_down, *, bm, tn):
    R, dff = act.shape
    D = w_down.shape[2]
    tn = min(tn, D)
    nj = D // tn

    def last(i, nu):
        return jnp.minimum(i, nu[0] - 1)

    blocks = [_nbytes((bm, dff), BF16), _nbytes((8, tn), F32), _nbytes((bm, tn), F32)]
    scratch = _nbytes((dff, tn), F32) + _nbytes((dff, tn), BF16) + _nbytes((bm, tn), F32)
    return pl.pallas_call(
        functools.partial(_expert_down_kernel, tn=tn, nj=nj, dff=dff),
        grid_spec=pltpu.PrefetchScalarGridSpec(
            num_scalar_prefetch=5,
            grid=(nj, R // bm),
            in_specs=[
                pl.BlockSpec((bm, dff), lambda j, i, be, f, ne, lg, nu: (last(i, nu), 0)),
                pl.BlockSpec(memory_space=pl.ANY),
                pl.BlockSpec((None, 1, tn), lambda j, i, be, f, ne, lg, nu: (be[last(i, nu)], 0, j)),
            ],
            out_specs=pl.BlockSpec((bm, tn // 2), lambda j, i, be, f, ne, lg, nu: (i, j)),
            scratch_shapes=[pltpu.VMEM((dff, tn), F32), pltpu.VMEM((dff, tn), BF16), pltpu.SemaphoreType.DMA((1,))],
        ),
        out_shape=jax.ShapeDtypeStruct((R, D // 2), jnp.uint32),
        compiler_params=_params(("arbitrary", "arbitrary"), blocks, scratch),
        name="moe_expert_down",
    )(*plan, n_used, act, w_down, b_down)


def _combine_kernel(dest_ref, h_ref, gt_ref, y_hbm, g_ref, o_ref, buf_ref, sem, *, tb, nblocks, pair_tile):
    i = pl.program_id(0)

    def start_block(blk):
        slot = lax.rem(blk, 2)

        def issue(r, c):
            for k in range(TOP_K):
                _row_copy(y_hbm, dest_ref[(blk * tb + r) * TOP_K + k], buf_ref.at[slot, k], r, sem.at[slot]).start()
            return c

        lax.fori_loop(0, tb, issue, 0, unroll=4)

    @pl.when(i == 0)
    def _():
        start_block(i)

    @pl.when(i + 1 < nblocks)
    def _():
        start_block(i + 1)

    slot = lax.rem(i, 2)
    for k in range(TOP_K):
        pltpu.make_async_copy(y_hbm.at[pl.ds(0, tb), :], buf_ref.at[slot, k], sem.at[slot]).wait()

    def body(r, c):
        sl = pl.ds(pl.multiple_of(r * NORM_ROWS, NORM_ROWS), NORM_ROWS)
        gt = gt_ref[sl, :]
        acc_a = acc_b = None
        for k in range(TOP_K):
            a, b = _unpack_bf16_pair(buf_ref[slot, k, sl, :])
            gk = gt[:, k:k + 1]
            acc_a = a * gk if acc_a is None else acc_a + a * gk
            acc_b = b * gk if acc_b is None else acc_b + b * gk
        pieces = []
        for j in range(acc_a.shape[1] // pair_tile):
            cs = slice(j * pair_tile, (j + 1) * pair_tile)
            pieces += [acc_a[:, cs], acc_b[:, cs]]
        hv = h_ref[sl, :] + jnp.concatenate(pieces, axis=1)
        ms = jnp.mean(hv * hv, axis=-1, keepdims=True)
        o_ref[sl, :] = hv * lax.rsqrt(ms + EPS) * g_ref[...]
        return c

    lax.fori_loop(0, tb // NORM_ROWS, body, 0)


def moe_combine(dest, h, top_gate, y, g, *, tb, pair_tile):
    T, D = h.shape
    blocks = [2 * _nbytes((tb, D), F32), _nbytes((tb, LANES), F32)]
    return pl.pallas_call(
        functools.partial(_combine_kernel, tb=tb, nblocks=T // tb, pair_tile=pair_tile),
        grid_spec=pltpu.PrefetchScalarGridSpec(
            num_scalar_prefetch=1,
            grid=(T // tb,),
            in_specs=[
                pl.BlockSpec((tb, D), lambda i, d: (i, 0)),
                pl.BlockSpec((tb, LANES), lambda i, d: (i, 0)),
                pl.BlockSpec(memory_space=pl.ANY),
                pl.BlockSpec((1, D), lambda i, d: (0, 0)),
            ],
            out_specs=pl.BlockSpec((tb, D), lambda i, d: (i, 0)),
            scratch_shapes=[pltpu.VMEM((2, TOP_K, tb, D // 2), jnp.uint32), pltpu.SemaphoreType.DMA((2,))],
        ),
        out_shape=jax.ShapeDtypeStruct((T, D), F32),
        compiler_params=_params(("arbitrary",), blocks, _nbytes((2, TOP_K, tb, D), F32)),
        name="moe_combine",
    )(dest, h, top_gate, y, g)


def _routing_tables(top_idx, rank, counts, *, bm):
    T = top_idx.shape[0]
    tk = T * TOP_K
    flat_e = top_idx.reshape(tk)
    padded = (counts + bm - 1) // bm * bm
    pad_end = jnp.cumsum(padded)
    pad_start = pad_end - padded
    dest = (pad_start[flat_e] + rank.reshape(tk)).astype(jnp.int32)
    nb = tk // bm + N_EXPERTS
    rows = nb * bm
    zero_start = jnp.maximum(pad_end - bm, 0).astype(jnp.int32)
    block_expert = jnp.minimum(
        jnp.searchsorted(pad_end, jnp.arange(nb, dtype=pad_end.dtype) * bm, side="right"), N_EXPERTS - 1
    ).astype(jnp.int32)
    n_used = (pad_end[-1] // bm).astype(jnp.int32).reshape(1)
    blk = jnp.arange(nb, dtype=jnp.int32)
    used = blk < n_used[0]
    be_used = jnp.where(used, block_expert, N_EXPERTS)
    prev = jnp.concatenate([jnp.full((1,), -1, jnp.int32), be_used[:-1]])
    first = jnp.logical_and(used, be_used != prev).astype(jnp.int32)
    nxt = jnp.searchsorted(be_used, be_used, side="right").astype(jnp.int32)
    is_last = (nxt >= n_used[0]).astype(jnp.int32)
    next_e = jnp.where(is_last == 1, block_expert[0], block_expert[jnp.minimum(nxt, nb - 1)]).astype(jnp.int32)
    plan = (block_expert, first, next_e, is_last)
    return dest, zero_start, plan, n_used, rows


def _rot_half(w):
    half = QK_ROPE // 2
    return jnp.concatenate([-w[..., half:], w[..., :half]], axis=-1)


def _layer(h, pos, invf, ln1_g, w_in, q_norm_g, w_q_up, kv_norm_g, w_kv_up, conv_w, a_log, dt_bias, gdn_norm_g,
           w_mla_o, w_gdn_o, w_out, ln2_g, router_w, router_b, w_gu, b_gu, w_down, b_down, out_g, *, B, S):
    T, D = h.shape
    H = MLA_HEADS
    gqkv = 2 * GDN_HEADS * GDN_DK + GDN_HEADS * GDN_DV
    gw = GDN_HEADS * GDN_DV
    o0 = Q_LORA + KV_LORA
    c_kr, c_qkv = o0, o0 + QK_ROPE
    c_gate = c_qkv + gqkv
    c_ab = c_gate + gw
    c_za = c_ab + 4 * GDN_HEADS
    c_zb = c_za + D

    w_kr = w_in[:, c_kr:c_qkv]
    w_kt = _rot_half(w_kr)
    small_cols = [w_in[:, :o0], w_kr, w_kr, w_kt, w_kt, w_in[:, c_ab:c_za]]
    n_small = 2048
    used = o0 + 4 * QK_ROPE + 4 * GDN_HEADS
    small_cols.append(jnp.zeros((D, n_small - used), F32))
    w_small = jnp.concatenate(small_cols, axis=1).astype(BF16)
    w_big = jnp.concatenate([w_in[:, c_qkv:c_gate], w_in[:, c_gate:c_ab], w_in[:, c_za:c_zb], w_in[:, c_zb:]],
                            axis=1).astype(BF16)
    ka_blk = o0 // LANES
    kb_blk = ka_blk + 1
    ab_blk = ka_blk + 2

    wq = w_q_up.reshape(Q_LORA, H, QK_NOPE + QK_ROPE)
    wq_rope = wq[:, :, QK_NOPE:]
    wq2 = jnp.concatenate([wq[:, :, :QK_NOPE], wq_rope, _rot_half(wq_rope)], axis=-1).reshape(Q_LORA, H * 2 * LANES)
    wq2 = wq2.astype(BF16)
    wkv = w_kv_up.reshape(KV_LORA, H, QK_NOPE + V_HEAD)
    wkv2 = jnp.concatenate([wkv[:, :, :QK_NOPE].reshape(KV_LORA, H * QK_NOPE),
                            wkv[:, :, QK_NOPE:].reshape(KV_LORA, H * V_HEAD)], axis=1).astype(BF16)

    small = norm_matmul(h, ln1_g, w_small, out_dtype=F32, tm=512, tn=1024, name="in_proj_small")
    big = norm_matmul(h, ln1_g, w_big, out_dtype=BF16, tm=512, tn=1024, name="in_proj_big")

    q2 = norm_matmul(small, q_norm_g, wq2, out_dtype=BF16, tm=1024, tn=1024, x_col_block=0, name="q_up")
    kv = norm_matmul(small, kv_norm_g, wkv2, out_dtype=BF16, tm=1024, tn=1024, x_col_block=Q_LORA // KV_LORA,
                     name="kv_up")
    score_scale = (QK_NOPE + QK_ROPE) ** -0.5 * LOG2_E
    kr2, cs = rope_prep(pos, invf, small, ka_blk, kb_blk, tm=1024, scale=score_scale)
    o_a = mla_attention(q2, cs, kv, kr2, B=B, S=S, tq=256, scale=score_scale)

    qkvc = gdn_conv(big, conv_w, B=B, S=S)
    pad = jnp.zeros((LANES - 2 * GDN_HEADS,), F32)
    alog_l = jnp.concatenate([a_log.reshape(-1), pad]).reshape(1, LANES)
    dtb_l = jnp.concatenate([dt_bias.reshape(-1), pad]).reshape(1, LANES)
    gp = gdn_gates(small, ab_blk, alog_l, dtb_l, tm=512)
    hg = GDN_HG
    ng = GDN_HEADS // hg
    nch = S // CHUNK
    nwin = S // GDN_WIN
    gp4 = gp[:, :4 * GDN_HEADS].reshape(B, S, 4, ng, hg)
    gcum = gp4[:, :, :2].reshape(B, nch, CHUNK, 2, ng, hg)
    g_end = jnp.stack([gcum[:, :, CHUNK - 1, 0], gcum[:, :, 0, 1]], axis=2)
    g_end = jnp.broadcast_to(g_end[:, :, None], (B, nch, CHUNK, 2, ng, hg)).reshape(B, S, 2, ng, hg)
    col = jnp.concatenate([gp4, g_end], axis=2).transpose(0, 3, 1, 2, 4).reshape(B, ng, S, 6 * hg)
    row = gp4[:, :, :2].reshape(B, nwin, GDN_WIN, 2, ng, hg).transpose(0, 4, 1, 3, 5, 2)
    row = row.reshape(B, ng, nwin, 2 * hg, GDN_WIN)
    o_b = gdn_scan(qkvc, big, gqkv // (hg * LANES), col, row, gdn_norm_g.reshape(1, GDN_DV), B=B, S=S)

    merged = gated_pair_matmul(o_a, w_mla_o.astype(BF16), o_b, w_gdn_o.astype(BF16), big,
                               (gqkv + gw) // 1024, (gqkv + gw + D) // 1024, tm=512, tn=1024, name="mixer_out")
    h1 = residual_matmul(merged, w_out.astype(BF16), h, tm=512, tn=1024, name="out_proj")

    rw = jnp.concatenate([router_w, jnp.zeros((D, LANES - N_EXPERTS), F32)], axis=1)
    rb = jnp.concatenate([router_b, jnp.zeros((LANES - N_EXPERTS,), F32)]).reshape(1, LANES)
    n2, top_idx, top_gate = moe_router(h1, ln2_g, rw, rb, tm=256)
    rank, counts = moe_rank(top_idx, tb=512)
    dest, zero_start, plan, n_used, rows = _routing_tables(
        top_idx[:, :TOP_K], rank[:, :TOP_K], counts[0, :N_EXPERTS].astype(jnp.int32), bm=MOE_BM)
    xs = moe_scatter_rows(dest, zero_start, n_used, n2, rows=rows, bm=MOE_BM, tpb=512)
    dff = w_gu.shape[2] // 2
    act = moe_expert_up(plan, n_used, xs, w_gu, b_gu.reshape(N_EXPERTS, 1, 2 * dff), bm=MOE_BM, tn=512)
    down_tn = min(2048, D)
    y = moe_expert_down(plan, n_used, act, w_down, b_down.reshape(N_EXPERTS, 1, D), bm=MOE_BM, tn=down_tn)
    return moe_combine(dest, h1, top_gate, y, out_g, tb=128, pair_tile=down_tn // 2)


def kernel(x, positions, ln1_g, w_in, q_norm_g, w_q_up, kv_norm_g, w_kv_up, gdn_conv_w, gdn_a_log, gdn_dt_bias,
           gdn_norm_g, w_mla_o, w_gdn_o, w_out, ln2_g, router_w, router_b, w_gu, b_gu, w_down, b_down, ln_f_g):
    B, S, D = x.shape
    depth = ln1_g.shape[0]
    assert depth == 1, "the final norm is fused into the last layer's expert combine"
    half = QK_ROPE // 2
    inv_freq = ROPE_THETA ** (-jnp.arange(0, QK_ROPE, 2, dtype=F32) / QK_ROPE)
    invf = jnp.tile(inv_freq, LANES // half).reshape(1, LANES)
    pos = positions.reshape(B * S, 1).astype(jnp.int32)
    h = x.reshape(B * S, D)
    l = 0
    out = _layer(h, pos, invf, ln1_g[l].reshape(1, D), w_in[l], q_norm_g[l].reshape(1, Q_LORA), w_q_up[l],
                 kv_norm_g[l].reshape(1, KV_LORA), w_kv_up[l], gdn_conv_w[l], gdn_a_log[l], gdn_dt_bias[l],
                 gdn_norm_g[l], w_mla_o[l], w_gdn_o[l], w_out[l], ln2_g[l].reshape(1, D), router_w[l], router_b[l],
                 w_gu[l], b_gu[l], w_down[l], b_down[l], ln_f_g.reshape(1, D), B=B, S=S)
    return out.reshape(B, S, D)
```

```python
import functools

import jax
import jax.numpy as jnp
from jax import lax
from jax.experimental import pallas as pl
from jax.experimental.pallas import tpu as pltpu

F32 = jnp.float32
BF16 = jnp.bfloat16

EPS = 1e-6
MLA_HEADS = 16
Q_LORA = 1024
KV_LORA = 512
QK_NOPE = 128
QK_ROPE = 64
V_HEAD = 128
ROPE_THETA = 10000.0
GDN_HEADS = 16
GDN_DK = 128
GDN_DV = 128
CONV_W = 5
CHUNK = 64
N_EXPERTS = 32
TOP_K = 4
SWIGLU_LIMIT = 7.0
SWIGLU_ALPHA = 1.702

LANES = 128
V7X_VMEM_BYTES = 64 * 1024 * 1024
VMEM_CAP = V7X_VMEM_BYTES - 8 * 1024 * 1024
LOG2_E = 1.4426950408889634
ATTN_HEADS_PER_STEP = 2
NORM_ROWS = 32
GDN_HG = 2
GDN_WIN = 256
MOE_BM = 512
HIGHEST = lax.Precision.HIGHEST


def _nbytes(shape, dtype):
    n = 1
    for s in shape:
        n *= s
    return n * jnp.dtype(dtype).itemsize


def _params(sem, blocks, scratch=0):
    need = 2 * sum(blocks) + scratch + 16 * 1024 * 1024
    return pltpu.CompilerParams(dimension_semantics=sem, vmem_limit_bytes=int(min(need, VMEM_CAP)))


def _mm(a, b):
    return jnp.dot(a, b, preferred_element_type=F32)


def _mm_nt(a, b):
    return lax.dot_general(a, b, (((1,), (1,)), ((), ())), preferred_element_type=F32)


def _mm_tn(a, b):
    return lax.dot_general(a, b, (((0,), (0,)), ((), ())), preferred_element_type=F32)


def _norm_rows_into(x_ref, g_ref, xn_ref, rows):
    def body(r, c):
        sl = pl.ds(pl.multiple_of(r * NORM_ROWS, NORM_ROWS), NORM_ROWS)
        xv = x_ref[sl, :].astype(F32)
        ms = jnp.mean(xv * xv, axis=-1, keepdims=True)
        xn_ref[sl, :] = (xv * lax.rsqrt(ms + EPS) * g_ref[...]).astype(BF16)
        return c

    lax.fori_loop(0, rows // NORM_ROWS, body, 0)


def _norm_matmul_kernel(x_ref, g_ref, w_ref, o_ref, xn_ref, *, tm):
    @pl.when(pl.program_id(1) == 0)
    def _():
        _norm_rows_into(x_ref, g_ref, xn_ref, tm)

    o_ref[...] = _mm(xn_ref[...], w_ref[...]).astype(o_ref.dtype)


def norm_matmul(x, g, w, *, out_dtype, tm, tn, x_col_block=0, name):
    T = x.shape[0]
    K, N = w.shape
    tm = min(tm, T)
    assert T % tm == 0 and N % tn == 0 and tm % NORM_ROWS == 0
    blocks = [_nbytes((tm, K), x.dtype), _nbytes((K, tn), BF16), _nbytes((tm, tn), out_dtype)]
    return pl.pallas_call(
        functools.partial(_norm_matmul_kernel, tm=tm),
        grid=(T // tm, N // tn),
        in_specs=[
            pl.BlockSpec((tm, K), lambda i, j: (i, x_col_block)),
            pl.BlockSpec((1, K), lambda i, j: (0, 0)),
            pl.BlockSpec((K, tn), lambda i, j: (0, j)),
        ],
        out_specs=pl.BlockSpec((tm, tn), lambda i, j: (i, j)),
        out_shape=jax.ShapeDtypeStruct((T, N), out_dtype),
        scratch_shapes=[pltpu.VMEM((tm, K), BF16)],
        compiler_params=_params(("parallel", "arbitrary"), blocks, _nbytes((tm, K), BF16)),
        name=name,
    )(x, g, w)


def _rmsnorm_cast_kernel(x_ref, g_ref, o_ref, *, tm):
    _norm_rows_into(x_ref, g_ref, o_ref, tm)


def rmsnorm_cast(x, g, *, tm):
    T, K = x.shape
    blocks = [_nbytes((tm, K), F32), _nbytes((tm, K), BF16)]
    return pl.pallas_call(
        functools.partial(_rmsnorm_cast_kernel, tm=tm),
        grid=(T // tm,),
        in_specs=[pl.BlockSpec((tm, K), lambda i: (i, 0)), pl.BlockSpec((1, K), lambda i: (0, 0))],
        out_specs=pl.BlockSpec((tm, K), lambda i: (i, 0)),
        out_shape=jax.ShapeDtypeStruct((T, K), BF16),
        compiler_params=_params(("parallel",), blocks),
        name="rmsnorm_cast",
    )(x, g)


def _matmul_kernel(x_ref, w_ref, o_ref):
    o_ref[...] = _mm(x_ref[...], w_ref[...]).astype(o_ref.dtype)


def matmul(x, w, *, out_dtype, tm, tn, name):
    T, K = x.shape
    N = w.shape[1]
    tm = min(tm, T)
    assert T % tm == 0 and N % tn == 0
    blocks = [_nbytes((tm, K), BF16), _nbytes((K, tn), BF16), _nbytes((tm, tn), out_dtype)]
    return pl.pallas_call(
        _matmul_kernel,
        grid=(T // tm, N // tn),
        in_specs=[pl.BlockSpec((tm, K), lambda i, j: (i, 0)), pl.BlockSpec((K, tn), lambda i, j: (0, j))],
        out_specs=pl.BlockSpec((tm, tn), lambda i, j: (i, j)),
        out_shape=jax.ShapeDtypeStruct((T, N), out_dtype),
        compiler_params=_params(("parallel", "parallel"), blocks),
        name=name,
    )(x, w)


def _gated_pair_kernel(a_ref, wa_ref, za_ref, b_ref, wb_ref, zb_ref, o_ref):
    ya = _mm(a_ref[...], wa_ref[...])
    yb = _mm(b_ref[...], wb_ref[...])
    ga = jax.nn.sigmoid(za_ref[...].astype(F32))
    gb = jax.nn.sigmoid(zb_ref[...].astype(F32))
    o_ref[...] = (ga * ya + gb * yb).astype(o_ref.dtype)


def gated_pair_matmul(a, wa, b, wb, z, za_col_block, zb_col_block, *, tm, tn, name):
    T, K = a.shape
    N = wa.shape[1]
    blocks = [2 * _nbytes((tm, K), BF16), 2 * _nbytes((K, tn), BF16), 3 * _nbytes((tm, tn), BF16)]
    return pl.pallas_call(
        _gated_pair_kernel,
        grid=(T // tm, N // tn),
        in_specs=[
            pl.BlockSpec((tm, K), lambda i, j: (i, 0)),
            pl.BlockSpec((K, tn), lambda i, j: (0, j)),
            pl.BlockSpec((tm, tn), lambda i, j: (i, za_col_block + j)),
            pl.BlockSpec((tm, K), lambda i, j: (i, 0)),
            pl.BlockSpec((K, tn), lambda i, j: (0, j)),
            pl.BlockSpec((tm, tn), lambda i, j: (i, zb_col_block + j)),
        ],
        out_specs=pl.BlockSpec((tm, tn), lambda i, j: (i, j)),
        out_shape=jax.ShapeDtypeStruct((T, N), BF16),
        compiler_params=_params(("parallel", "parallel"), blocks),
        name=name,
    )(a, wa, z, b, wb, z)


def _residual_matmul_kernel(x_ref, w_ref, r_ref, o_ref):
    o_ref[...] = r_ref[...] + _mm(x_ref[...], w_ref[...])


def residual_matmul(x, w, r, *, tm, tn, name):
    T, K = x.shape
    N = w.shape[1]
    blocks = [_nbytes((tm, K), BF16), _nbytes((K, tn), BF16), 2 * _nbytes((tm, tn), F32)]
    return pl.pallas_call(
        _residual_matmul_kernel,
        grid=(T // tm, N // tn),
        in_specs=[
            pl.BlockSpec((tm, K), lambda i, j: (i, 0)),
            pl.BlockSpec((K, tn), lambda i, j: (0, j)),
            pl.BlockSpec((tm, tn), lambda i, j: (i, j)),
        ],
        out_specs=pl.BlockSpec((tm, tn), lambda i, j: (i, j)),
        out_shape=jax.ShapeDtypeStruct((T, N), F32),
        compiler_params=_params(("parallel", "parallel"), blocks),
        name=name,
    )(x, w, r)


def _rope_prep_kernel(pos_ref, invf_ref, ka_ref, kb_ref, kr_ref, cs_ref, *, scale):
    ang = pos_ref[...].astype(F32) * invf_ref[...]
    c = jnp.cos(ang)
    s = jnp.sin(ang)
    kr_ref[...] = (ka_ref[...] * c + kb_ref[...] * s).astype(kr_ref.dtype)
    lane = lax.broadcasted_iota(jnp.int32, c.shape, 1)
    cs_ref[...] = jnp.where(lane < QK_ROPE, c, s) * scale


def rope_prep(pos, invf, small, ka_col_block, kb_col_block, *, tm, scale):
    T = pos.shape[0]
    blocks = [4 * _nbytes((tm, LANES), F32), _nbytes((tm, LANES), F32)]
    return pl.pallas_call(
        functools.partial(_rope_prep_kernel, scale=scale),
        grid=(T // tm,),
        in_specs=[
            pl.BlockSpec((tm, 1), lambda i: (i, 0)),
            pl.BlockSpec((1, LANES), lambda i: (0, 0)),
            pl.BlockSpec((tm, LANES), lambda i: (i, ka_col_block)),
            pl.BlockSpec((tm, LANES), lambda i: (i, kb_col_block)),
        ],
        out_specs=[pl.BlockSpec((tm, LANES), lambda i: (i, 0)), pl.BlockSpec((tm, LANES), lambda i: (i, 0))],
        out_shape=[jax.ShapeDtypeStruct((T, LANES), BF16), jax.ShapeDtypeStruct((T, LANES), F32)],
        compiler_params=_params(("parallel",), blocks),
        name="rope_prep",
    )(pos, invf, small, small)


def _attention_kernel(q_ref, cs_ref, kn_ref, kr_ref, v_ref, o_ref, kcat_ref, *, scale, nh):
    @pl.when(pl.program_id(2) == 0)
    def _():
        for a in range(nh):
            kcat_ref[a, :, :QK_NOPE] = kn_ref[:, a * QK_NOPE:(a + 1) * QK_NOPE]
            kcat_ref[a, :, QK_NOPE:] = kr_ref[...]

    heads = range(nh)
    qq = []
    for a in heads:
        q = q_ref[:, a * 2 * LANES:(a + 1) * 2 * LANES]
        qn = (q[:, :QK_NOPE].astype(F32) * scale).astype(BF16)
        qr = (q[:, QK_NOPE:].astype(F32) * cs_ref[...]).astype(BF16)
        qq.append(jnp.concatenate([qn, qr], axis=1))
    s = [_mm_nt(qq[a], kcat_ref[a]) for a in heads]
    m = [jnp.max(s[a], axis=-1, keepdims=True) for a in heads]
    p = [jnp.exp2(s[a] - m[a]) for a in heads]
    l = [jnp.sum(p[a], axis=-1, keepdims=True) for a in heads]
    o = [_mm(p[a].astype(BF16), v_ref[:, a * V_HEAD:(a + 1) * V_HEAD]) for a in heads]
    for a in heads:
        o_ref[:, a * V_HEAD:(a + 1) * V_HEAD] = (o[a] / l[a]).astype(o_ref.dtype)


def mla_attention(q2, cs, kv, kr2, *, B, S, tq, scale):
    T = B * S
    H = MLA_HEADS
    nh = ATTN_HEADS_PER_STEP
    nq = S // tq
    wq, wk = nh * 2 * LANES, nh * LANES
    blocks = [_nbytes((tq, wq), BF16), _nbytes((tq, LANES), F32), 2 * _nbytes((S, wk), BF16),
              _nbytes((S, LANES), BF16), _nbytes((tq, wk), BF16)]
    scratch = _nbytes((nh, S, 2 * LANES), BF16) + 3 * nh * _nbytes((tq, S), F32)
    return pl.pallas_call(
        functools.partial(_attention_kernel, scale=scale, nh=nh),
        grid=(B, H // nh, nq),
        in_specs=[
            pl.BlockSpec((tq, wq), lambda b, h, i: (b * nq + i, h)),
            pl.BlockSpec((tq, LANES), lambda b, h, i: (b * nq + i, 0)),
            pl.BlockSpec((S, wk), lambda b, h, i: (b, h)),
            pl.BlockSpec((S, LANES), lambda b, h, i: (b, 0)),
            pl.BlockSpec((S, wk), lambda b, h, i: (b, H // nh + h)),
        ],
        out_specs=pl.BlockSpec((tq, wk), lambda b, h, i: (b * nq + i, h)),
        out_shape=jax.ShapeDtypeStruct((T, H * V_HEAD), BF16),
        scratch_shapes=[pltpu.VMEM((nh, S, 2 * LANES), BF16)],
        compiler_params=_params(("parallel", "parallel", "arbitrary"), blocks, scratch),
        name="mla_attention",
    )(q2, cs, kv, kr2, kv)


def _conv_kernel(x_ref, w_ref, o_ref, xp_ref, *, S, rows, n_qk_blocks, n_q_blocks):
    pad = 8
    cb = pl.program_id(1)
    xp_ref[0:pad, :] = jnp.zeros((pad, LANES), F32)
    xp_ref[S + pad:S + 2 * pad, :] = jnp.zeros((pad, LANES), F32)
    for t in range(S // rows):
        xp_ref[pad + t * rows:pad + (t + 1) * rows, :] = x_ref[t * rows:(t + 1) * rows, :].astype(F32)
    w = w_ref[...]
    half = CONV_W // 2
    q_scale = jnp.where(cb < n_q_blocks, GDN_DK ** -0.5, 1.0).astype(F32)
    for t in range(S // rows):
        acc = jnp.zeros((rows, LANES), F32)
        for k in range(CONV_W):
            off = pad + t * rows + k - half
            acc = acc + xp_ref[off:off + rows, :] * w[k:k + 1, :]
        y = acc * jax.nn.sigmoid(acc)
        ss = jnp.sum(y * y, axis=-1, keepdims=True)
        f = jnp.where(cb < n_qk_blocks, lax.rsqrt(ss + EPS) * q_scale, 1.0)
        o_ref[t * rows:(t + 1) * rows, :] = (y * f).astype(o_ref.dtype)


def gdn_conv(big, conv_w, *, B, S):
    T = B * S
    C = conv_w.shape[1]
    ncb = C // LANES
    n_q = GDN_HEADS * GDN_DK // LANES
    blocks = [2 * _nbytes((S, LANES), BF16), _nbytes((8, LANES), F32)]
    return pl.pallas_call(
        functools.partial(_conv_kernel, S=S, rows=256, n_qk_blocks=2 * n_q, n_q_blocks=n_q),
        grid=(B, ncb),
        in_specs=[
            pl.BlockSpec((S, LANES), lambda b, c: (b, c)),
            pl.BlockSpec((CONV_W, LANES), lambda b, c: (0, c)),
        ],
        out_specs=pl.BlockSpec((S, LANES), lambda b, c: (b, c)),
        out_shape=jax.ShapeDtypeStruct((T, C), BF16),
        scratch_shapes=[pltpu.VMEM((S + 16, LANES), F32)],
        compiler_params=_params(("parallel", "parallel"), blocks, _nbytes((S + 16, LANES), F32)),
        name="gdn_conv",
    )(big, conv_w)


def _gates_kernel(ab_ref, alog_ref, dtb_ref, o_ref, *, tm):
    H = GDN_HEADS
    x = ab_ref[...]
    z = x + dtb_ref[...]
    softplus = jnp.maximum(z, 0.0) + jnp.log(1.0 + jnp.exp(-jnp.abs(z)))
    g = -jnp.exp(alog_ref[...]) * softplus
    beta = jax.nn.sigmoid(x)
    lane = lax.broadcasted_iota(jnp.int32, (CHUNK, LANES), 1)
    ri = lax.broadcasted_iota(jnp.int32, (CHUNK, CHUNK), 0)
    ci = lax.broadcasted_iota(jnp.int32, (CHUNK, CHUNK), 1)
    tril = (ri >= ci).astype(F32)
    triu = (ri <= ci).astype(F32)
    for t in range(tm // CHUNK):
        sl = slice(t * CHUNK, (t + 1) * CHUNK)
        gc = g[sl]
        pre = jnp.dot(tril, gc, precision=HIGHEST, preferred_element_type=F32)
        suf = jnp.dot(triu, gc, precision=HIGHEST, preferred_element_type=F32)
        cum = jnp.where(lane < H, pre, suf)
        o_ref[sl, :] = jnp.where(lane < 2 * H, cum, beta[sl])


def gdn_gates(small, ab_col_block, alog_l, dtb_l, *, tm):
    T = small.shape[0]
    blocks = [2 * _nbytes((tm, LANES), F32)]
    return pl.pallas_call(
        functools.partial(_gates_kernel, tm=tm),
        grid=(T // tm,),
        in_specs=[
            pl.BlockSpec((tm, LANES), lambda i: (i, ab_col_block)),
            pl.BlockSpec((1, LANES), lambda i: (0, 0)),
            pl.BlockSpec((1, LANES), lambda i: (0, 0)),
        ],
        out_specs=pl.BlockSpec((tm, LANES), lambda i: (i, 0)),
        out_shape=jax.ShapeDtypeStruct((T, LANES), F32),
        compiler_params=_params(("parallel",), blocks),
        name="gdn_gates",
    )(small, alog_l, dtb_l)


def _window_masks(reverse):
    w = GDN_WIN
    ri = lax.broadcasted_iota(jnp.int32, (w, w), 0)
    ci = lax.broadcasted_iota(jnp.int32, (w, w), 1)
    same = jnp.right_shift(ri, 6) == jnp.right_shift(ci, 6)
    order_incl = (ri <= ci) if reverse else (ri >= ci)
    order_strict = (ri < ci) if reverse else (ri > ci)
    return dict(
        incl=jnp.logical_and(same, order_incl),
        strict=jnp.logical_and(same, order_strict),
        b16=jnp.right_shift(ri, 4) == jnp.right_shift(ci, 4),
        b32=jnp.right_shift(ri, 5) == jnp.right_shift(ci, 5),
        diag=ri == ci,
    )


def _delta_prep_stages(chains, out):
    w = GDN_WIN

    def bf(x):
        return x.astype(BF16)

    def cat(x, y):
        return jnp.concatenate([bf(x), bf(y)], axis=1)

    n = range(len(chains))
    gram = [_mm_nt(jnp.concatenate([c["q"], c["k"]], axis=0), c["k"]) for c in chains]
    yield
    decay = [jnp.exp(jnp.where(c["m"]["incl"], c["gc"] - c["gr"], -jnp.inf)) for c in chains]
    a = [jnp.where(chains[i]["m"]["strict"], gram[i][w:] * chains[i]["bc"] * decay[i], 0.0) for i in n]
    nd = [jnp.where(chains[i]["m"]["b16"], -a[i], 0.0) for i in n]
    n2 = [_mm(bf(nd[i]), bf(nd[i])) for i in n]
    yield
    p = [jnp.where(chains[i]["m"]["diag"], 1.0, nd[i]) for i in n]
    r = [_mm(bf(n2[i]), cat(p[i], n2[i])) for i in n]
    yield
    p = [p[i] + r[i][:, :w] for i in n]
    n4 = [r[i][:, w:] for i in n]
    r = [_mm(bf(n4[i]), cat(p[i], n4[i])) for i in n]
    yield
    p = [p[i] + r[i][:, :w] for i in n]
    n8 = [r[i][:, w:] for i in n]
    r = [_mm(bf(n8[i]), bf(p[i])) for i in n]
    yield
    di = [p[i] + r[i] for i in n]
    lo = [jnp.where(chains[i]["m"]["b16"], 0.0, a[i]) for i in n]
    nt = [-_mm(bf(di[i]), bf(lo[i])) for i in n]
    yield
    r = [_mm(bf(nt[i]), cat(nt[i], di[i])) for i in n]
    yield
    s1 = [di[i] + r[i][:, w:] for i in n]
    t = [s1[i] + _mm(bf(r[i][:, :w]), bf(s1[i])) for i in n]
    yield
    eg = [jnp.exp(c["gc"]) for c in chains]
    kf = [c["k"].astype(F32) for c in chains]
    rhs = [cat(kf[i] * (chains[i]["bc"] * eg[i]), chains[i]["v"].astype(F32) * chains[i]["bc"]) for i in n]
    wu = [bf(_mm(bf(t[i]), rhs[i])) for i in n]
    yield
    col_chunk = jnp.right_shift(lax.broadcasted_iota(jnp.int32, (GDN_DK, w), 1), 6)
    lhs = []
    for i in n:
        c = chains[i]
        qk = jnp.where(c["m"]["incl"], gram[i][:w] * decay[i], 0.0)
        k_end_t = jnp.transpose(kf[i] * jnp.exp(c["glc"] - c["gc"]))
        spread = [jnp.where(col_chunk == j, k_end_t, 0.0) for j in range(w // CHUNK)]
        lhs.append(bf(jnp.concatenate([qk] + spread, axis=0)))
    res = [_mm(lhs[i], wu[i]) for i in n]
    yield
    for i in n:
        c = chains[i]
        x = res[i][:w]
        qp = c["q"].astype(F32) * eg[i] - x[:, :GDN_DK]
        out.append((bf(qp), x[:, GDN_DK:], bf(res[i][w:])))


def _gdn_kernel(q_ref, k_ref, v_ref, gate_ref, col_ref, row_ref, ng_ref, o_ref,
                st_ref, of_ref, ob_ref, qp_ref, y_ref, *, S, hg):
    nwin = S // GDN_WIN
    cpw = GDN_WIN // CHUNK
    st_ref[...] = jnp.zeros(st_ref.shape, F32)

    def chains():
        for d in range(2):
            for hl in range(hg):
                yield d, hl, d * hg + hl

    def win_start(w, d):
        wi = w if d == 0 else nwin - 1 - w
        return wi, pl.multiple_of(wi * GDN_WIN, GDN_WIN)

    def prep_stages(w, slot):
        masks = (_window_masks(False), _window_masks(True))
        ins = []
        for d, hl, ch in chains():
            wi, r0 = win_start(w, d)
            hs = slice(hl * LANES, (hl + 1) * LANES)
            colv = col_ref[pl.ds(r0, GDN_WIN), :]
            rowv = row_ref[wi]
            gi, bi, li = ch, 2 * hg + ch, 4 * hg + ch
            ins.append(dict(q=q_ref[pl.ds(r0, GDN_WIN), hs], k=k_ref[pl.ds(r0, GDN_WIN), hs],
                            v=v_ref[pl.ds(r0, GDN_WIN), hs], gc=colv[:, gi:gi + 1], gr=rowv[gi:gi + 1, :],
                            bc=colv[:, bi:bi + 1], glc=colv[:, li:li + 1], m=masks[d]))
        out = []
        yield from _delta_prep_stages(ins, out)
        for (d, hl, ch), (qp, o0, y) in zip(chains(), out):
            _, r0 = win_start(w, d)
            qp_ref[slot, ch] = qp
            y_ref[slot, ch] = y
            (of_ref if d == 0 else ob_ref)[pl.ds(r0, GDN_WIN), pl.ds(hl * LANES, LANES)] = o0

    def seq_stages(w, slot):
        states = [st_ref[ch] for _, _, ch in chains()]
        for step in range(cpw):
            for i, (d, hl, ch) in enumerate(chains()):
                _, r0 = win_start(w, d)
                hs = slice(hl * LANES, (hl + 1) * LANES)
                dst = of_ref if d == 0 else ob_ref
                j = step if d == 0 else cpw - 1 - step
                rj = pl.multiple_of(r0 + j * CHUNK, CHUNK)
                yj = y_ref[slot, ch, j * GDN_DK:(j + 1) * GDN_DK, :]
                lhs = jnp.concatenate([yj[:, :GDN_DK], qp_ref[slot, ch, j * CHUNK:(j + 1) * CHUNK, :]], axis=0)
                r = _mm(lhs, states[i].astype(BF16))
                glv = col_ref[pl.ds(rj, 1), :][:, 4 * hg + ch:4 * hg + ch + 1]
                dst[pl.ds(rj, CHUNK), hs] = dst[pl.ds(rj, CHUNK), hs] + r[GDN_DK:]
                states[i] = states[i] * jnp.exp(glv) - r[:GDN_DK] + yj[:, GDN_DK:].astype(F32)
            yield
        for i, (_, _, ch) in enumerate(chains()):
            st_ref[ch] = states[i]

    def run(*gens):
        live = list(gens)
        while live:
            for g in list(live):
                try:
                    next(g)
                except StopIteration:
                    live.remove(g)

    run(prep_stages(0, 0))

    def body(w, carry):
        slot = lax.rem(w, 2)
        run(seq_stages(w, slot), prep_stages(w + 1, 1 - slot))
        return carry

    lax.fori_loop(0, nwin - 1, body, 0)
    run(seq_stages(nwin - 1, (nwin - 1) % 2))

    rows = 256
    for t in range(S // rows):
        rs = slice(t * rows, (t + 1) * rows)
        for hl in range(hg):
            hs = slice(hl * LANES, (hl + 1) * LANES)
            o = of_ref[rs, hs] + ob_ref[rs, hs]
            ms = jnp.mean(o * o, axis=-1, keepdims=True)
            gate = gate_ref[rs, hs].astype(F32)
            y = o * lax.rsqrt(ms + EPS) * ng_ref[...]
            o_ref[rs, hs] = (y * (gate * jax.nn.sigmoid(gate))).astype(o_ref.dtype)


def gdn_scan(qkvc, big, gate_col_block0, col, row, norm_g, *, B, S):
    T = B * S
    hg = GDN_HG
    ngroups = GDN_HEADS // hg
    w = hg * LANES
    nwin = S // GDN_WIN
    cpw = GDN_WIN // CHUNK
    assert S % GDN_WIN == 0 and nwin >= 2
    blocks = [5 * _nbytes((S, w), BF16), _nbytes((S, LANES), F32), _nbytes((nwin, 8, GDN_WIN), F32)]
    scratch = (_nbytes((2 * hg, GDN_DK, GDN_DV), F32) + 2 * _nbytes((S, w), F32)
               + _nbytes((2, 2 * hg, GDN_WIN, GDN_DK), BF16) + _nbytes((2, 2 * hg, cpw * GDN_DK, 2 * GDN_DV), BF16))
    return pl.pallas_call(
        functools.partial(_gdn_kernel, S=S, hg=hg),
        grid=(B, ngroups),
        in_specs=[
            pl.BlockSpec((S, w), lambda b, g: (b, g)),
            pl.BlockSpec((S, w), lambda b, g: (b, ngroups + g)),
            pl.BlockSpec((S, w), lambda b, g: (b, 2 * ngroups + g)),
            pl.BlockSpec((S, w), lambda b, g: (b, gate_col_block0 + g)),
            pl.BlockSpec((None, None, S, 6 * hg), lambda b, g: (b, g, 0, 0)),
            pl.BlockSpec((None, None, nwin, 2 * hg, GDN_WIN), lambda b, g: (b, g, 0, 0, 0)),
            pl.BlockSpec((1, GDN_DV), lambda b, g: (0, 0)),
        ],
        out_specs=pl.BlockSpec((S, w), lambda b, g: (b, g)),
        out_shape=jax.ShapeDtypeStruct((T, GDN_HEADS * GDN_DV), BF16),
        scratch_shapes=[
            pltpu.VMEM((2 * hg, GDN_DK, GDN_DV), F32),
            pltpu.VMEM((S, w), F32),
            pltpu.VMEM((S, w), F32),
            pltpu.VMEM((2, 2 * hg, GDN_WIN, GDN_DK), BF16),
            pltpu.VMEM((2, 2 * hg, cpw * GDN_DK, 2 * GDN_DV), BF16),
        ],
        compiler_params=_params(("parallel", "parallel"), blocks, scratch),
        name="gdn_scan",
    )(qkvc, qkvc, qkvc, big, col, row, norm_g)


def _pack_bf16_pair(a, b):
    hi = lax.bitcast_convert_type(a.astype(BF16).astype(F32), jnp.uint32)
    lo = lax.bitcast_convert_type(b.astype(BF16).astype(F32), jnp.uint32)
    return hi | lax.shift_right_logical(lo, jnp.uint32(16))


def _unpack_bf16_pair(w):
    a = lax.bitcast_convert_type(w & jnp.uint32(0xFFFF0000), F32)
    b = lax.bitcast_convert_type(lax.shift_left(w, jnp.uint32(16)), F32)
    return a, b


def _router_kernel(h_ref, g_ref, rw_ref, rb_ref, n_ref, idx_ref, gate_ref):
    x = h_ref[...]
    ms = jnp.mean(x * x, axis=-1, keepdims=True)
    n = x * lax.rsqrt(ms + EPS) * g_ref[...]
    half = n.shape[1] // 2
    n_ref[...] = _pack_bf16_pair(n[:, :half], n[:, half:])
    logits = jnp.dot(n, rw_ref[...], precision=HIGHEST, preferred_element_type=F32) + rb_ref[...]
    lane = lax.broadcasted_iota(jnp.int32, logits.shape, 1).astype(F32)
    neg = -jnp.inf
    l = jnp.where(lane < N_EXPERTS, logits, neg)
    vals, idxs = [], []
    for _ in range(TOP_K):
        m = jnp.max(l, axis=-1, keepdims=True)
        idx = jnp.min(jnp.where(l == m, lane, float(LANES)), axis=-1, keepdims=True)
        vals.append(m)
        idxs.append(idx)
        l = jnp.where(lane == idx, neg, l)
    es = [jnp.exp(v - vals[0]) for v in vals]
    tot = es[0] + es[1] + es[2] + es[3]
    iout = jnp.zeros(logits.shape, F32)
    gout = jnp.zeros(logits.shape, F32)
    for k in range(TOP_K):
        iout = jnp.where(lane == k, idxs[k], iout)
        gout = jnp.where(lane == k, es[k] / tot, gout)
    idx_ref[...] = iout.astype(jnp.int32)
    gate_ref[...] = gout


def moe_router(h, g, rw, rb, *, tm):
    T, D = h.shape
    blocks = [2 * _nbytes((tm, D), F32), _nbytes((D, LANES), F32), 2 * _nbytes((tm, LANES), F32)]
    return pl.pallas_call(
        _router_kernel,
        grid=(T // tm,),
        in_specs=[
            pl.BlockSpec((tm, D), lambda i: (i, 0)),
            pl.BlockSpec((1, D), lambda i: (0, 0)),
            pl.BlockSpec((D, LANES), lambda i: (0, 0)),
            pl.BlockSpec((1, LANES), lambda i: (0, 0)),
        ],
        out_specs=[
            pl.BlockSpec((tm, D // 2), lambda i: (i, 0)),
            pl.BlockSpec((tm, LANES), lambda i: (i, 0)),
            pl.BlockSpec((tm, LANES), lambda i: (i, 0)),
        ],
        out_shape=[
            jax.ShapeDtypeStruct((T, D // 2), jnp.uint32),
            jax.ShapeDtypeStruct((T, LANES), jnp.int32),
            jax.ShapeDtypeStruct((T, LANES), F32),
        ],
        compiler_params=_params(("parallel",), blocks, 4 * _nbytes((tm, D), F32)),
        name="moe_router",
    )(h, g, rw, rb)


def _row_copy(src_hbm, row, dst_ref, dst_row, sem):
    return pltpu.make_async_copy(src_hbm.at[pl.ds(row, 1), :], dst_ref.at[pl.ds(dst_row, 1), :], sem)


def _rank_kernel(idx_ref, rank_ref, cnt_ref, carry_ref, *, tb):
    @pl.when(pl.program_id(0) == 0)
    def _():
        carry_ref[...] = jnp.zeros(carry_ref.shape, F32)

    idx = idx_ref[...]
    lane = lax.broadcasted_iota(jnp.int32, idx.shape, 1)
    hits = [lane == idx[:, k:k + 1] for k in range(TOP_K)]
    onehot = jnp.zeros(idx.shape, F32)
    for h in hits:
        onehot = onehot + jnp.where(h, 1.0, 0.0)
    ri = lax.broadcasted_iota(jnp.int32, (tb, tb), 0)
    ci = lax.broadcasted_iota(jnp.int32, (tb, tb), 1)
    tril = jnp.where(ri >= ci, 1.0, 0.0).astype(BF16)
    incl = _mm(tril, onehot.astype(BF16))
    carry = carry_ref[0:1, :]
    before = carry + incl - onehot
    out = jnp.zeros(idx.shape, F32)
    for k in range(TOP_K):
        rk = jnp.sum(jnp.where(hits[k], before, 0.0), axis=-1, keepdims=True)
        out = jnp.where(lane == k, rk, out)
    rank_ref[...] = out.astype(jnp.int32)
    total = jnp.broadcast_to(carry + incl[tb - 1:tb, :], carry_ref.shape)
    carry_ref[...] = total
    cnt_ref[...] = total


def moe_rank(top_idx, *, tb):
    T = top_idx.shape[0]
    blocks = [2 * _nbytes((tb, LANES), F32), _nbytes((8, LANES), F32)]
    return pl.pallas_call(
        functools.partial(_rank_kernel, tb=tb),
        grid=(T // tb,),
        in_specs=[pl.BlockSpec((tb, LANES), lambda i: (i, 0))],
        out_specs=[pl.BlockSpec((tb, LANES), lambda i: (i, 0)), pl.BlockSpec((8, LANES), lambda i: (0, 0))],
        out_shape=[jax.ShapeDtypeStruct((T, LANES), jnp.int32), jax.ShapeDtypeStruct((8, LANES), F32)],
        scratch_shapes=[pltpu.VMEM((8, LANES), F32)],
        compiler_params=_params(("arbitrary",), blocks, 4 * _nbytes((tb, tb), F32)),
        name="moe_rank",
    )(top_idx)


def _scatter_rows_kernel(dest_ref, zs_ref, nu_ref, src_ref, out_hbm, zero_ref, sem, zsem, *, tpb, bm, nb):
    i = pl.program_id(0)

    @pl.when(i == 0)
    def _():
        zero_ref[...] = jnp.zeros(zero_ref.shape, zero_ref.dtype)

        def zero_block(row0):
            return pltpu.make_async_copy(zero_ref, out_hbm.at[pl.ds(pl.multiple_of(row0, bm), bm), :], zsem)

        def start_unused(b, c):
            zero_block(b * bm).start()
            return c

        def wait_unused(b, c):
            zero_block(b * bm).wait()
            return c

        for e in range(N_EXPERTS):
            zero_block(zs_ref[e]).start()
        lax.fori_loop(nu_ref[0], nb, start_unused, 0)
        for e in range(N_EXPERTS):
            zero_block(zs_ref[e]).wait()
        lax.fori_loop(nu_ref[0], nb, wait_unused, 0)

    def issue(t, c):
        tok = i * tpb + t
        for k in range(TOP_K):
            pltpu.make_async_copy(src_ref.at[pl.ds(t, 1), :], out_hbm.at[pl.ds(dest_ref[tok * TOP_K + k], 1), :],
                                  sem).start()
        return c

    lax.fori_loop(0, tpb, issue, 0, unroll=4)

    for k in range(TOP_K):
        pltpu.make_async_copy(src_ref, out_hbm.at[pl.ds(0, tpb), :], sem).wait()


def moe_scatter_rows(dest, zero_start, n_used, src, *, rows, bm, tpb):
    T, W = src.shape
    blocks = [_nbytes((tpb, W), src.dtype)]
    return pl.pallas_call(
        functools.partial(_scatter_rows_kernel, tpb=tpb, bm=bm, nb=rows // bm),
        grid_spec=pltpu.PrefetchScalarGridSpec(
            num_scalar_prefetch=3,
            grid=(T // tpb,),
            in_specs=[pl.BlockSpec((tpb, W), lambda i, d, z, nu: (i, 0))],
            out_specs=pl.BlockSpec(memory_space=pl.ANY),
            scratch_shapes=[pltpu.VMEM((bm, W), src.dtype), pltpu.SemaphoreType.DMA, pltpu.SemaphoreType.DMA],
        ),
        out_shape=jax.ShapeDtypeStruct((rows, W), src.dtype),
        compiler_params=_params(("arbitrary",), blocks, _nbytes((bm, W), src.dtype)),
        name="moe_scatter_rows",
    )(dest, zero_start, n_used, src)


CAST_ROWS = 256


def _stream_expert_weights(plan, j, i, nj, copies, convert):
    be_ref, first_ref, next_e_ref, last_ref = plan

    @pl.when(jnp.logical_and(j == 0, i == 0))
    def _():
        for c in copies(be_ref[0], 0):
            c.start()

    @pl.when(first_ref[i] == 1)
    def _():
        for c in copies(be_ref[i], j):
            c.wait()
        convert()
        is_last = last_ref[i]

        @pl.when(jnp.logical_or(is_last == 0, j + 1 < nj))
        def _():
            for c in copies(next_e_ref[i], j + is_last):
                c.start()


def _cast_rows(src_ref, dst_ref, col0, width, rows):
    def body(r, c):
        sl = pl.ds(pl.multiple_of(r * CAST_ROWS, CAST_ROWS), CAST_ROWS)
        dst_ref[sl, col0:col0 + width] = src_ref[sl, :].astype(BF16)
        return c

    lax.fori_loop(0, rows // CAST_ROWS, body, 0)


def _expert_up_kernel(be_ref, first_ref, next_e_ref, last_ref, nu_ref, x_ref, w_hbm, bg_ref, bu_ref, o_ref,
                      stage_ref, wbf_ref, sem, *, tn, nj, dff, D):
    j = pl.program_id(0)
    i = pl.program_id(1)

    def copies(e, jj):
        return [pltpu.make_async_copy(w_hbm.at[e, :, pl.ds(pl.multiple_of(h * dff + jj * tn, tn), tn)],
                                      stage_ref.at[h], sem.at[h]) for h in range(2)]

    def convert():
        for h in range(2):
            _cast_rows(stage_ref.at[h], wbf_ref, h * tn, tn, D)

    _stream_expert_weights((be_ref, first_ref, next_e_ref, last_ref), j, i, nj, copies, convert)

    @pl.when(i < nu_ref[0])
    def _():
        xa, xb = _unpack_bf16_pair(x_ref[...])
        half = D // 2
        gu = _mm(xa.astype(BF16), wbf_ref[:half, :]) + _mm(xb.astype(BF16), wbf_ref[half:, :])
        gate = jnp.minimum(gu[:, :tn] + bg_ref[...], SWIGLU_LIMIT)
        up = jnp.clip(gu[:, tn:] + bu_ref[...], -SWIGLU_LIMIT, SWIGLU_LIMIT)
        act = (up + 1.0) * gate * jax.nn.sigmoid(SWIGLU_ALPHA * gate)
        o_ref[...] = act.astype(o_ref.dtype)

    @pl.when(i >= nu_ref[0])
    def _():
        o_ref[...] = jnp.zeros(o_ref.shape, o_ref.dtype)


def moe_expert_up(plan, n_used, xs, w_gu, b_gu, *, bm, tn):
    R = xs.shape[0]
    D = w_gu.shape[1]
    dff = w_gu.shape[2] // 2
    tn = min(tn, dff)
    nj = dff // tn
    nb = R // bm

    def last(i, nu):
        return jnp.minimum(i, nu[0] - 1)

    blocks = [_nbytes((bm, D), BF16), 2 * _nbytes((8, tn), F32), _nbytes((bm, tn), BF16)]
    scratch = _nbytes((2, D, tn), F32) + _nbytes((D, 2 * tn), BF16) + 4 * _nbytes((bm, tn), F32)
    return pl.pallas_call(
        functools.partial(_expert_up_kernel, tn=tn, nj=nj, dff=dff, D=D),
        grid_spec=pltpu.PrefetchScalarGridSpec(
            num_scalar_prefetch=5,
            grid=(nj, nb),
            in_specs=[
                pl.BlockSpec((bm, D // 2), lambda j, i, be, f, ne, lg, nu: (last(i, nu), 0)),
                pl.BlockSpec(memory_space=pl.ANY),
                pl.BlockSpec((None, 1, tn), lambda j, i, be, f, ne, lg, nu: (be[last(i, nu)], 0, j)),
                pl.BlockSpec((None, 1, tn), lambda j, i, be, f, ne, lg, nu: (be[last(i, nu)], 0, nj + j)),
            ],
            out_specs=pl.BlockSpec((bm, tn), lambda j, i, be, f, ne, lg, nu: (i, j)),
            scratch_shapes=[pltpu.VMEM((2, D, tn), F32), pltpu.VMEM((D, 2 * tn), BF16), pltpu.SemaphoreType.DMA((2,))],
        ),
        out_shape=jax.ShapeDtypeStruct((R, dff), BF16),
        compiler_params=_params(("arbitrary", "arbitrary"), blocks, scratch),
        name="moe_expert_up",
    )(*plan, n_used, xs, w_gu, b_gu, b_gu)


def _expert_down_kernel(be_ref, first_ref, next_e_ref, last_ref, nu_ref, a_ref, w_hbm, b_ref, o_ref,
                        stage_ref, wbf_ref, sem, *, tn, nj, dff):
    j = pl.program_id(0)
    i = pl.program_id(1)

    def copies(e, jj):
        return [pltpu.make_async_copy(w_hbm.at[e, :, pl.ds(pl.multiple_of(jj * tn, tn), tn)], stage_ref, sem.at[0])]

    def convert():
        _cast_rows(stage_ref, wbf_ref, 0, tn, dff)

    _stream_expert_weights((be_ref, first_ref, next_e_ref, last_ref), j, i, nj, copies, convert)

    @pl.when(i < nu_ref[0])
    def _():
        y = _mm(a_ref[...], wbf_ref[...]) + b_ref[...]
        o_ref[...] = _pack_bf16_pair(y[:, :tn // 2], y[:, tn // 2:])

    @pl.when(i >= nu_ref[0])
    def _():
        o_ref[...] = jnp.zeros(o_ref.shape, o_ref.dtype)


def moe_expert_down(plan, n_used, act, w_down, b_down, *, bm, tn):
    R, dff = act.shape
    D = w_down.shape[2]
    tn = min(tn, D)
    nj = D // tn

    def last(i, nu):
        return jnp.minimum(i, nu[0] - 1)

    blocks = [_nbytes((bm, dff), BF16), _nbytes((8, tn), F32), _nbytes((bm, tn), F32)]
    scratch = _nbytes((dff, tn), F32) + _nbytes((dff, tn), BF16) + _nbytes((bm, tn), F32)
    return pl.pallas_call(
        functools.partial(_expert_down_kernel, tn=tn, nj=nj, dff=dff),
        grid_spec=pltpu.PrefetchScalarGridSpec(
            num_scalar_prefetch=5,
            grid=(nj, R // bm),
            in_specs=[
                pl.BlockSpec((bm, dff), lambda j, i, be, f, ne, lg, nu: (last(i, nu), 0)),
                pl.BlockSpec(memory_space=pl.ANY),
                pl.BlockSpec((None, 1, tn), lambda j, i, be, f, ne, lg, nu: (be[last(i, nu)], 0, j)),
            ],
            out_specs=pl.BlockSpec((bm, tn // 2), lambda j, i, be, f, ne, lg, nu: (i, j)),
            scratch_shapes=[pltpu.VMEM((dff, tn), F32), pltpu.VMEM((dff, tn), BF16), pltpu.SemaphoreType.DMA((1,))],
        ),
        out_shape=jax.ShapeDtypeStruct((R, D // 2), jnp.uint32),
        compiler_params=_params(("arbitrary", "arbitrary"), blocks, scratch),
        name="moe_expert_down",
    )(*plan, n_used, act, w_down, b_down)


def _combine_kernel(dest_ref, h_ref, gt_ref, y_hbm, g_ref, o_ref, buf_ref, sem, *, tb, nblocks, pair_tile):
    i = pl.program_id(0)

    def start_block(blk):
        slot = lax.rem(blk, 2)

        def issue(r, c):
            for k in range(TOP_K):
                _row_copy(y_hbm, dest_ref[(blk * tb + r) * TOP_K + k], buf_ref.at[slot, k], r, sem.at[slot]).start()
            return c

        lax.fori_loop(0, tb, issue, 0, unroll=4)

    @pl.when(i == 0)
    def _():
        start_block(i)

    @pl.when(i + 1 < nblocks)
    def _():
        start_block(i + 1)

    slot = lax.rem(i, 2)
    for k in range(TOP_K):
        pltpu.make_async_copy(y_hbm.at[pl.ds(0, tb), :], buf_ref.at[slot, k], sem.at[slot]).wait()

    def body(r, c):
        sl = pl.ds(pl.multiple_of(r * NORM_ROWS, NORM_ROWS), NORM_ROWS)
        gt = gt_ref[sl, :]
        acc_a = acc_b = None
        for k in range(TOP_K):
            a, b = _unpack_bf16_pair(buf_ref[slot, k, sl, :])
            gk = gt[:, k:k + 1]
            acc_a = a * gk if acc_a is None else acc_a + a * gk
            acc_b = b * gk if acc_b is None else acc_b + b * gk
        pieces = []
        for j in range(acc_a.shape[1] // pair_tile):
            cs = slice(j * pair_tile, (j + 1) * pair_tile)
            pieces += [acc_a[:, cs], acc_b[:, cs]]
        hv = h_ref[sl, :] + jnp.concatenate(pieces, axis=1)
        ms = jnp.mean(hv * hv, axis=-1, keepdims=True)
        o_ref[sl, :] = hv * lax.rsqrt(ms + EPS) * g_ref[...]
        return c

    lax.fori_loop(0, tb // NORM_ROWS, body, 0)


def moe_combine(dest, h, top_gate, y, g, *, tb, pair_tile):
    T, D = h.shape
    blocks = [2 * _nbytes((tb, D), F32), _nbytes((tb, LANES), F32)]
    return pl.pallas_call(
        functools.partial(_combine_kernel, tb=tb, nblocks=T // tb, pair_tile=pair_tile),
        grid_spec=pltpu.PrefetchScalarGridSpec(
            num_scalar_prefetch=1,
            grid=(T // tb,),
            in_specs=[
                pl.BlockSpec((tb, D), lambda i, d: (i, 0)),
                pl.BlockSpec((tb, LANES), lambda i, d: (i, 0)),
                pl.BlockSpec(memory_space=pl.ANY),
                pl.BlockSpec((1, D), lambda i, d: (0, 0)),
            ],
            out_specs=pl.BlockSpec((tb, D), lambda i, d: (i, 0)),
            scratch_shapes=[pltpu.VMEM((2, TOP_K, tb, D // 2), jnp.uint32), pltpu.SemaphoreType.DMA((2,))],
        ),
        out_shape=jax.ShapeDtypeStruct((T, D), F32),
        compiler_params=_params(("arbitrary",), blocks, _nbytes((2, TOP_K, tb, D), F32)),
        name="moe_combine",
    )(dest, h, top_gate, y, g)


def _routing_tables(top_idx, rank, counts, *, bm):
    T = top_idx.shape[0]
    tk = T * TOP_K
    flat_e = top_idx.reshape(tk)
    padded = (counts + bm - 1) // bm * bm
    pad_end = jnp.cumsum(padded)
    pad_start = pad_end - padded
    dest = (pad_start[flat_e] + rank.reshape(tk)).astype(jnp.int32)
    nb = tk // bm + N_EXPERTS
    rows = nb * bm
    zero_start = jnp.maximum(pad_end - bm, 0).astype(jnp.int32)
    block_expert = jnp.minimum(
        jnp.searchsorted(pad_end, jnp.arange(nb, dtype=pad_end.dtype) * bm, side="right"), N_EXPERTS - 1
    ).astype(jnp.int32)
    n_used = (pad_end[-1] // bm).astype(jnp.int32).reshape(1)
    blk = jnp.arange(nb, dtype=jnp.int32)
    used = blk < n_used[0]
    be_used = jnp.where(used, block_expert, N_EXPERTS)
    prev = jnp.concatenate([jnp.full((1,), -1, jnp.int32), be_used[:-1]])
    first = jnp.logical_and(used, be_used != prev).astype(jnp.int32)
    nxt = jnp.searchsorted(be_used, be_used, side="right").astype(jnp.int32)
    is_last = (nxt >= n_used[0]).astype(jnp.int32)
    next_e = jnp.where(is_last == 1, block_expert[0], block_expert[jnp.minimum(nxt, nb - 1)]).astype(jnp.int32)
    plan = (block_expert, first, next_e, is_last)
    return dest, zero_start, plan, n_used, rows


def _rot_half(w):
    half = QK_ROPE // 2
    return jnp.concatenate([-w[..., half:], w[..., :half]], axis=-1)


def _layer(h, pos, invf, ln1_g, w_in, q_norm_g, w_q_up, kv_norm_g, w_kv_up, conv_w, a_log, dt_bias, gdn_norm_g,
           w_mla_o, w_gdn_o, w_out, ln2_g, router_w, router_b, w_gu, b_gu, w_down, b_down, out_g, *, B, S):
    T, D = h.shape
    H = MLA_HEADS
    gqkv = 2 * GDN_HEADS * GDN_DK + GDN_HEADS * GDN_DV
    gw = GDN_HEADS * GDN_DV
    o0 = Q_LORA + KV_LORA
    c_kr, c_qkv = o0, o0 + QK_ROPE
    c_gate = c_qkv + gqkv
    c_ab = c_gate + gw
    c_za = c_ab + 4 * GDN_HEADS
    c_zb = c_za + D

    w_kr = w_in[:, c_kr:c_qkv]
    w_kt = _rot_half(w_kr)
    small_cols = [w_in[:, :o0], w_kr, w_kr, w_kt, w_kt, w_in[:, c_ab:c_za]]
    n_small = 2048
    used = o0 + 4 * QK_ROPE + 4 * GDN_HEADS
    small_cols.append(jnp.zeros((D, n_small - used), F32))
    w_small = jnp.concatenate(small_cols, axis=1).astype(BF16)
    w_big = jnp.concatenate([w_in[:, c_qkv:c_gate], w_in[:, c_gate:c_ab], w_in[:, c_za:c_zb], w_in[:, c_zb:]],
                            axis=1).astype(BF16)
    ka_blk = o0 // LANES
    kb_blk = ka_blk + 1
    ab_blk = ka_blk + 2

    wq = w_q_up.reshape(Q_LORA, H, QK_NOPE + QK_ROPE)
    wq_rope = wq[:, :, QK_NOPE:]
    wq2 = jnp.concatenate([wq[:, :, :QK_NOPE], wq_rope, _rot_half(wq_rope)], axis=-1).reshape(Q_LORA, H * 2 * LANES)
    wq2 = wq2.astype(BF16)
    wkv = w_kv_up.reshape(KV_LORA, H, QK_NOPE + V_HEAD)
    wkv2 = jnp.concatenate([wkv[:, :, :QK_NOPE].reshape(KV_LORA, H * QK_NOPE),
                            wkv[:, :, QK_NOPE:].reshape(KV_LORA, H * V_HEAD)], axis=1).astype(BF16)

    n1 = rmsnorm_cast(h, ln1_g, tm=512)
    small = matmul(n1, w_small, out_dtype=F32, tm=1024, tn=1024, name="in_proj_small")
    big = matmul(n1, w_big, out_dtype=BF16, tm=1024, tn=1024, name="in_proj_big")

    q2 = norm_matmul(small, q_norm_g, wq2, out_dtype=BF16, tm=1024, tn=1024, x_col_block=0, name="q_up")
    kv = norm_matmul(small, kv_norm_g, wkv2, out_dtype=BF16, tm=1024, tn=1024, x_col_block=Q_LORA // KV_LORA,
                     name="kv_up")
    score_scale = (QK_NOPE + QK_ROPE) ** -0.5 * LOG2_E
    kr2, cs = rope_prep(pos, invf, small, ka_blk, kb_blk, tm=1024, scale=score_scale)
    o_a = mla_attention(q2, cs, kv, kr2, B=B, S=S, tq=256, scale=score_scale)

    qkvc = gdn_conv(big, conv_w, B=B, S=S)
    pad = jnp.zeros((LANES - 2 * GDN_HEADS,), F32)
    alog_l = jnp.concatenate([a_log.reshape(-1), pad]).reshape(1, LANES)
    dtb_l = jnp.concatenate([dt_bias.reshape(-1), pad]).reshape(1, LANES)
    gp = gdn_gates(small, ab_blk, alog_l, dtb_l, tm=512)
    hg = GDN_HG
    ng = GDN_HEADS // hg
    nch = S // CHUNK
    nwin = S // GDN_WIN
    gp4 = gp[:, :4 * GDN_HEADS].reshape(B, S, 4, ng, hg)
    gcum = gp4[:, :, :2].reshape(B, nch, CHUNK, 2, ng, hg)
    g_end = jnp.stack([gcum[:, :, CHUNK - 1, 0], gcum[:, :, 0, 1]], axis=2)
    g_end = jnp.broadcast_to(g_end[:, :, None], (B, nch, CHUNK, 2, ng, hg)).reshape(B, S, 2, ng, hg)
    col = jnp.concatenate([gp4, g_end], axis=2).transpose(0, 3, 1, 2, 4).reshape(B, ng, S, 6 * hg)
    row = gp4[:, :, :2].reshape(B, nwin, GDN_WIN, 2, ng, hg).transpose(0, 4, 1, 3, 5, 2)
    row = row.reshape(B, ng, nwin, 2 * hg, GDN_WIN)
    o_b = gdn_scan(qkvc, big, gqkv // (hg * LANES), col, row, gdn_norm_g.reshape(1, GDN_DV), B=B, S=S)

    merged = gated_pair_matmul(o_a, w_mla_o.astype(BF16), o_b, w_gdn_o.astype(BF16), big,
                               (gqkv + gw) // 1024, (gqkv + gw + D) // 1024, tm=512, tn=1024, name="mixer_out")
    h1 = residual_matmul(merged, w_out.astype(BF16), h, tm=512, tn=1024, name="out_proj")

    rw = jnp.concatenate([router_w, jnp.zeros((D, LANES - N_EXPERTS), F32)], axis=1)
    rb = jnp.concatenate([router_b, jnp.zeros((LANES - N_EXPERTS,), F32)]).reshape(1, LANES)
    n2, top_idx, top_gate = moe_router(h1, ln2_g, rw, rb, tm=256)
    rank, counts = moe_rank(top_idx, tb=512)
    dest, zero_start, plan, n_used, rows = _routing_tables(
        top_idx[:, :TOP_K], rank[:, :TOP_K], counts[0, :N_EXPERTS].astype(jnp.int32), bm=MOE_BM)
    xs = moe_scatter_rows(dest, zero_start, n_used, n2, rows=rows, bm=MOE_BM, tpb=512)
    dff = w_gu.shape[2] // 2
    act = moe_expert_up(plan, n_used, xs, w_gu, b_gu.reshape(N_EXPERTS, 1, 2 * dff), bm=MOE_BM, tn=512)
    down_tn = min(2048, D)
    y = moe_expert_down(plan, n_used, act, w_down, b_down.reshape(N_EXPERTS, 1, D), bm=MOE_BM, tn=down_tn)
    return moe_combine(dest, h1, top_gate, y, out_g, tb=128, pair_tile=down_tn // 2)


def kernel(x, positions, ln1_g, w_in, q_norm_g, w_q_up, kv_norm_g, w_kv_up, gdn_conv_w, gdn_a_log, gdn_dt_bias,
           gdn_norm_g, w_mla_o, w_gdn_o, w_out, ln2_g, router_w, router_b, w_gu, b_gu, w_down, b_down, ln_f_g):
    B, S, D = x.shape
    depth = ln1_g.shape[0]
    assert depth == 1, "the final norm is fused into the last layer's expert combine"
    half = QK_ROPE // 2
    inv_freq = ROPE_THETA ** (-jnp.arange(0, QK_ROPE, 2, dtype=F32) / QK_ROPE)
    invf = jnp.tile(inv_freq, LANES // half).reshape(1, LANES)
    pos = positions.reshape(B * S, 1).astype(jnp.int32)
    h = x.reshape(B * S, D)
    l = 0
    out = _layer(h, pos, invf, ln1_g[l].reshape(1, D), w_in[l], q_norm_g[l].reshape(1, Q_LORA), w_q_up[l],
                 kv_norm_g[l].reshape(1, KV_LORA), w_kv_up[l], gdn_conv_w[l], gdn_a_log[l], gdn_dt_bias[l],
                 gdn_norm_g[l], w_mla_o[l], w_gdn_o[l], w_out[l], ln2_g[l].reshape(1, D), router_w[l], router_b[l],
                 w_gu[l], b_gu[l], w_down[l], b_down[l], ln_f_g.reshape(1, D), B=B, S=S)
    return out.reshape(B, S, D)
```

```python
import functools

import jax
import jax.numpy as jnp
from jax import lax
from jax.experimental import pallas as pl
from jax.experimental.pallas import tpu as pltpu

F32 = jnp.float32
BF16 = jnp.bfloat16

EPS = 1e-6
MLA_HEADS = 16
Q_LORA = 1024
KV_LORA = 512
QK_NOPE = 128
QK_ROPE = 64
V_HEAD = 128
ROPE_THETA = 10000.0
GDN_HEADS = 16
GDN_DK = 128
GDN_DV = 128
CONV_W = 5
CHUNK = 64
N_EXPERTS = 32
TOP_K = 4
SWIGLU_LIMIT = 7.0
SWIGLU_ALPHA = 1.702

LANES = 128
V7X_VMEM_BYTES = 64 * 1024 * 1024
VMEM_CAP = V7X_VMEM_BYTES - 8 * 1024 * 1024
LOG2_E = 1.4426950408889634
ATTN_HEADS_PER_STEP = 2
NORM_ROWS = 32
GDN_HG = 2
GDN_WIN = 256
MOE_BM = 512
HIGHEST = lax.Precision.HIGHEST


def _nbytes(shape, dtype):
    n = 1
    for s in shape:
        n *= s
    return n * jnp.dtype(dtype).itemsize


def _params(sem, blocks, scratch=0):
    need = 2 * sum(blocks) + scratch + 16 * 1024 * 1024
    return pltpu.CompilerParams(dimension_semantics=sem, vmem_limit_bytes=int(min(need, VMEM_CAP)))


def _mm(a, b):
    return jnp.dot(a, b, preferred_element_type=F32)


def _mm_nt(a, b):
    return lax.dot_general(a, b, (((1,), (1,)), ((), ())), preferred_element_type=F32)


def _mm_tn(a, b):
    return lax.dot_general(a, b, (((0,), (0,)), ((), ())), preferred_element_type=F32)


def _norm_rows_into(x_ref, g_ref, xn_ref, rows):
    def body(r, c):
        sl = pl.ds(pl.multiple_of(r * NORM_ROWS, NORM_ROWS), NORM_ROWS)
        xv = x_ref[sl, :].astype(F32)
        ms = jnp.mean(xv * xv, axis=-1, keepdims=True)
        xn_ref[sl, :] = (xv * lax.rsqrt(ms + EPS) * g_ref[...]).astype(BF16)
        return c

    lax.fori_loop(0, rows // NORM_ROWS, body, 0)


def _norm_matmul_kernel(x_ref, g_ref, w_ref, o_ref, xn_ref, *, tm):
    @pl.when(pl.program_id(1) == 0)
    def _():
        _norm_rows_into(x_ref, g_ref, xn_ref, tm)

    o_ref[...] = _mm(xn_ref[...], w_ref[...]).astype(o_ref.dtype)


def norm_matmul(x, g, w, *, out_dtype, tm, tn, x_col_block=0, name):
    T = x.shape[0]
    K, N = w.shape
    tm = min(tm, T)
    assert T % tm == 0 and N % tn == 0 and tm % NORM_ROWS == 0
    blocks = [_nbytes((tm, K), x.dtype), _nbytes((K, tn), BF16), _nbytes((tm, tn), out_dtype)]
    return pl.pallas_call(
        functools.partial(_norm_matmul_kernel, tm=tm),
        grid=(T // tm, N // tn),
        in_specs=[
            pl.BlockSpec((tm, K), lambda i, j: (i, x_col_block)),
            pl.BlockSpec((1, K), lambda i, j: (0, 0)),
            pl.BlockSpec((K, tn), lambda i, j: (0, j)),
        ],
        out_specs=pl.BlockSpec((tm, tn), lambda i, j: (i, j)),
        out_shape=jax.ShapeDtypeStruct((T, N), out_dtype),
        scratch_shapes=[pltpu.VMEM((tm, K), BF16)],
        compiler_params=_params(("parallel", "arbitrary"), blocks, _nbytes((tm, K), BF16)),
        name=name,
    )(x, g, w)


def _rmsnorm_cast_kernel(x_ref, g_ref, o_ref, *, tm):
    _norm_rows_into(x_ref, g_ref, o_ref, tm)


def rmsnorm_cast(x, g, *, tm):
    T, K = x.shape
    blocks = [_nbytes((tm, K), F32), _nbytes((tm, K), BF16)]
    return pl.pallas_call(
        functools.partial(_rmsnorm_cast_kernel, tm=tm),
        grid=(T // tm,),
        in_specs=[pl.BlockSpec((tm, K), lambda i: (i, 0)), pl.BlockSpec((1, K), lambda i: (0, 0))],
        out_specs=pl.BlockSpec((tm, K), lambda i: (i, 0)),
        out_shape=jax.ShapeDtypeStruct((T, K), BF16),
        compiler_params=_params(("parallel",), blocks),
        name="rmsnorm_cast",
    )(x, g)


def _matmul_kernel(x_ref, w_ref, o_ref):
    o_ref[...] = _mm(x_ref[...], w_ref[...]).astype(o_ref.dtype)


def matmul(x, w, *, out_dtype, tm, tn, name):
    T, K = x.shape
    N = w.shape[1]
    tm = min(tm, T)
    assert T % tm == 0 and N % tn == 0
    blocks = [_nbytes((tm, K), BF16), _nbytes((K, tn), BF16), _nbytes((tm, tn), out_dtype)]
    return pl.pallas_call(
        _matmul_kernel,
        grid=(T // tm, N // tn),
        in_specs=[pl.BlockSpec((tm, K), lambda i, j: (i, 0)), pl.BlockSpec((K, tn), lambda i, j: (0, j))],
        out_specs=pl.BlockSpec((tm, tn), lambda i, j: (i, j)),
        out_shape=jax.ShapeDtypeStruct((T, N), out_dtype),
        compiler_params=_params(("parallel", "parallel"), blocks),
        name=name,
    )(x, w)


def _gated_pair_kernel(a_ref, wa_ref, za_ref, b_ref, wb_ref, zb_ref, o_ref):
    ya = _mm(a_ref[...], wa_ref[...])
    yb = _mm(b_ref[...], wb_ref[...])
    ga = jax.nn.sigmoid(za_ref[...].astype(F32))
    gb = jax.nn.sigmoid(zb_ref[...].astype(F32))
    o_ref[...] = (ga * ya + gb * yb).astype(o_ref.dtype)


def gated_pair_matmul(a, wa, b, wb, z, za_col_block, zb_col_block, *, tm, tn, name):
    T, K = a.shape
    N = wa.shape[1]
    blocks = [2 * _nbytes((tm, K), BF16), 2 * _nbytes((K, tn), BF16), 3 * _nbytes((tm, tn), BF16)]
    return pl.pallas_call(
        _gated_pair_kernel,
        grid=(T // tm, N // tn),
        in_specs=[
            pl.BlockSpec((tm, K), lambda i, j: (i, 0)),
            pl.BlockSpec((K, tn), lambda i, j: (0, j)),
            pl.BlockSpec((tm, tn), lambda i, j: (i, za_col_block + j)),
            pl.BlockSpec((tm, K), lambda i, j: (i, 0)),
            pl.BlockSpec((K, tn), lambda i, j: (0, j)),
            pl.BlockSpec((tm, tn), lambda i, j: (i, zb_col_block + j)),
        ],
        out_specs=pl.BlockSpec((tm, tn), lambda i, j: (i, j)),
        out_shape=jax.ShapeDtypeStruct((T, N), BF16),
        compiler_params=_params(("parallel", "parallel"), blocks),
        name=name,
    )(a, wa, z, b, wb, z)


def _residual_matmul_kernel(x_ref, w_ref, r_ref, o_ref):
    o_ref[...] = r_ref[...] + _mm(x_ref[...], w_ref[...])


def residual_matmul(x, w, r, *, tm, tn, name):
    T, K = x.shape
    N = w.shape[1]
    blocks = [_nbytes((tm, K), BF16), _nbytes((K, tn), BF16), 2 * _nbytes((tm, tn), F32)]
    return pl.pallas_call(
        _residual_matmul_kernel,
        grid=(T // tm, N // tn),
        in_specs=[
            pl.BlockSpec((tm, K), lambda i, j: (i, 0)),
            pl.BlockSpec((K, tn), lambda i, j: (0, j)),
            pl.BlockSpec((tm, tn), lambda i, j: (i, j)),
        ],
        out_specs=pl.BlockSpec((tm, tn), lambda i, j: (i, j)),
        out_shape=jax.ShapeDtypeStruct((T, N), F32),
        compiler_params=_params(("parallel", "parallel"), blocks),
        name=name,
    )(x, w, r)


def _rope_prep_kernel(pos_ref, invf_ref, ka_ref, kb_ref, kr_ref, cs_ref, *, scale):
    ang = pos_ref[...].astype(F32) * invf_ref[...]
    c = jnp.cos(ang)
    s = jnp.sin(ang)
    kr_ref[...] = (ka_ref[...] * c + kb_ref[...] * s).astype(kr_ref.dtype)
    lane = lax.broadcasted_iota(jnp.int32, c.shape, 1)
    cs_ref[...] = jnp.where(lane < QK_ROPE, c, s) * scale


def rope_prep(pos, invf, small, ka_col_block, kb_col_block, *, tm, scale):
    T = pos.shape[0]
    blocks = [4 * _nbytes((tm, LANES), F32), _nbytes((tm, LANES), F32)]
    return pl.pallas_call(
        functools.partial(_rope_prep_kernel, scale=scale),
        grid=(T // tm,),
        in_specs=[
            pl.BlockSpec((tm, 1), lambda i: (i, 0)),
            pl.BlockSpec((1, LANES), lambda i: (0, 0)),
            pl.BlockSpec((tm, LANES), lambda i: (i, ka_col_block)),
            pl.BlockSpec((tm, LANES), lambda i: (i, kb_col_block)),
        ],
        out_specs=[pl.BlockSpec((tm, LANES), lambda i: (i, 0)), pl.BlockSpec((tm, LANES), lambda i: (i, 0))],
        out_shape=[jax.ShapeDtypeStruct((T, LANES), BF16), jax.ShapeDtypeStruct((T, LANES), F32)],
        compiler_params=_params(("parallel",), blocks),
        name="rope_prep",
    )(pos, invf, small, small)


def _attention_kernel(q_ref, cs_ref, kn_ref, kr_ref, v_ref, o_ref, kcat_ref, *, scale, nh):
    @pl.when(pl.program_id(2) == 0)
    def _():
        for a in range(nh):
            kcat_ref[a, :, :QK_NOPE] = kn_ref[:, a * QK_NOPE:(a + 1) * QK_NOPE]
            kcat_ref[a, :, QK_NOPE:] = kr_ref[...]

    heads = range(nh)
    qq = []
    for a in heads:
        q = q_ref[:, a * 2 * LANES:(a + 1) * 2 * LANES]
        qn = (q[:, :QK_NOPE].astype(F32) * scale).astype(BF16)
        qr = (q[:, QK_NOPE:].astype(F32) * cs_ref[...]).astype(BF16)
        qq.append(jnp.concatenate([qn, qr], axis=1))
    s = [_mm_nt(qq[a], kcat_ref[a]) for a in heads]
    m = [jnp.max(s[a], axis=-1, keepdims=True) for a in heads]
    p = [jnp.exp2(s[a] - m[a]) for a in heads]
    l = [jnp.sum(p[a], axis=-1, keepdims=True) for a in heads]
    o = [_mm(p[a].astype(BF16), v_ref[:, a * V_HEAD:(a + 1) * V_HEAD]) for a in heads]
    for a in heads:
        o_ref[:, a * V_HEAD:(a + 1) * V_HEAD] = (o[a] / l[a]).astype(o_ref.dtype)


def mla_attention(q2, cs, kv, kr2, *, B, S, tq, scale):
    T = B * S
    H = MLA_HEADS
    nh = ATTN_HEADS_PER_STEP
    nq = S // tq
    wq, wk = nh * 2 * LANES, nh * LANES
    blocks = [_nbytes((tq, wq), BF16), _nbytes((tq, LANES), F32), 2 * _nbytes((S, wk), BF16),
              _nbytes((S, LANES), BF16), _nbytes((tq, wk), BF16)]
    scratch = _nbytes((nh, S, 2 * LANES), BF16) + 3 * nh * _nbytes((tq, S), F32)
    return pl.pallas_call(
        functools.partial(_attention_kernel, scale=scale, nh=nh),
        grid=(B, H // nh, nq),
        in_specs=[
            pl.BlockSpec((tq, wq), lambda b, h, i: (b * nq + i, h)),
            pl.BlockSpec((tq, LANES), lambda b, h, i: (b * nq + i, 0)),
            pl.BlockSpec((S, wk), lambda b, h, i: (b, h)),
            pl.BlockSpec((S, LANES), lambda b, h, i: (b, 0)),
            pl.BlockSpec((S, wk), lambda b, h, i: (b, H // nh + h)),
        ],
        out_specs=pl.BlockSpec((tq, wk), lambda b, h, i: (b * nq + i, h)),
        out_shape=jax.ShapeDtypeStruct((T, H * V_HEAD), BF16),
        scratch_shapes=[pltpu.VMEM((nh, S, 2 * LANES), BF16)],
        compiler_params=_params(("parallel", "parallel", "arbitrary"), blocks, scratch),
        name="mla_attention",
    )(q2, cs, kv, kr2, kv)


def _conv_kernel(x_ref, w_ref, o_ref, xp_ref, *, S, rows, n_qk_blocks, n_q_blocks):
    pad = 8
    cb = pl.program_id(1)
    xp_ref[0:pad, :] = jnp.zeros((pad, LANES), F32)
    xp_ref[S + pad:S + 2 * pad, :] = jnp.zeros((pad, LANES), F32)
    for t in range(S // rows):
        xp_ref[pad + t * rows:pad + (t + 1) * rows, :] = x_ref[t * rows:(t + 1) * rows, :].astype(F32)
    w = w_ref[...]
    half = CONV_W // 2
    q_scale = jnp.where(cb < n_q_blocks, GDN_DK ** -0.5, 1.0).astype(F32)
    for t in range(S // rows):
        acc = jnp.zeros((rows, LANES), F32)
        for k in range(CONV_W):
            off = pad + t * rows + k - half
            acc = acc + xp_ref[off:off + rows, :] * w[k:k + 1, :]
        y = acc * jax.nn.sigmoid(acc)
        ss = jnp.sum(y * y, axis=-1, keepdims=True)
        f = jnp.where(cb < n_qk_blocks, lax.rsqrt(ss + EPS) * q_scale, 1.0)
        o_ref[t * rows:(t + 1) * rows, :] = (y * f).astype(o_ref.dtype)


def gdn_conv(big, conv_w, *, B, S):
    T = B * S
    C = conv_w.shape[1]
    ncb = C // LANES
    n_q = GDN_HEADS * GDN_DK // LANES
    blocks = [2 * _nbytes((S, LANES), BF16), _nbytes((8, LANES), F32)]
    return pl.pallas_call(
        functools.partial(_conv_kernel, S=S, rows=256, n_qk_blocks=2 * n_q, n_q_blocks=n_q),
        grid=(B, ncb),
        in_specs=[
            pl.BlockSpec((S, LANES), lambda b, c: (b, c)),
            pl.BlockSpec((CONV_W, LANES), lambda b, c: (0, c)),
        ],
        out_specs=pl.BlockSpec((S, LANES), lambda b, c: (b, c)),
        out_shape=jax.ShapeDtypeStruct((T, C), BF16),
        scratch_shapes=[pltpu.VMEM((S + 16, LANES), F32)],
        compiler_params=_params(("parallel", "parallel"), blocks, _nbytes((S + 16, LANES), F32)),
        name="gdn_conv",
    )(big, conv_w)


def _gates_kernel(ab_ref, alog_ref, dtb_ref, o_ref, *, tm):
    H = GDN_HEADS
    x = ab_ref[...]
    z = x + dtb_ref[...]
    softplus = jnp.maximum(z, 0.0) + jnp.log(1.0 + jnp.exp(-jnp.abs(z)))
    g = -jnp.exp(alog_ref[...]) * softplus
    beta = jax.nn.sigmoid(x)
    lane = lax.broadcasted_iota(jnp.int32, (CHUNK, LANES), 1)
    ri = lax.broadcasted_iota(jnp.int32, (CHUNK, CHUNK), 0)
    ci = lax.broadcasted_iota(jnp.int32, (CHUNK, CHUNK), 1)
    tril = (ri >= ci).astype(F32)
    triu = (ri <= ci).astype(F32)
    for t in range(tm // CHUNK):
        sl = slice(t * CHUNK, (t + 1) * CHUNK)
        gc = g[sl]
        pre = jnp.dot(tril, gc, precision=HIGHEST, preferred_element_type=F32)
        suf = jnp.dot(triu, gc, precision=HIGHEST, preferred_element_type=F32)
        cum = jnp.where(lane < H, pre, suf)
        o_ref[sl, :] = jnp.where(lane < 2 * H, cum, beta[sl])


def gdn_gates(small, ab_col_block, alog_l, dtb_l, *, tm):
    T = small.shape[0]
    blocks = [2 * _nbytes((tm, LANES), F32)]
    return pl.pallas_call(
        functools.partial(_gates_kernel, tm=tm),
        grid=(T // tm,),
        in_specs=[
            pl.BlockSpec((tm, LANES), lambda i: (i, ab_col_block)),
            pl.BlockSpec((1, LANES), lambda i: (0, 0)),
            pl.BlockSpec((1, LANES), lambda i: (0, 0)),
        ],
        out_specs=pl.BlockSpec((tm, LANES), lambda i: (i, 0)),
        out_shape=jax.ShapeDtypeStruct((T, LANES), F32),
        compiler_params=_params(("parallel",), blocks),
        name="gdn_gates",
    )(small, alog_l, dtb_l)


def _window_masks(reverse):
    w = GDN_WIN
    ri = lax.broadcasted_iota(jnp.int32, (w, w), 0)
    ci = lax.broadcasted_iota(jnp.int32, (w, w), 1)
    same = jnp.right_shift(ri, 6) == jnp.right_shift(ci, 6)
    order_incl = (ri <= ci) if reverse else (ri >= ci)
    order_strict = (ri < ci) if reverse else (ri > ci)
    return dict(
        incl=jnp.logical_and(same, order_incl),
        strict=jnp.logical_and(same, order_strict),
        b16=jnp.right_shift(ri, 4) == jnp.right_shift(ci, 4),
        b32=jnp.right_shift(ri, 5) == jnp.right_shift(ci, 5),
        diag=ri == ci,
    )


def _delta_prep_stages(chains, out):
    w = GDN_WIN

    def bf(x):
        return x.astype(BF16)

    def cat(x, y):
        return jnp.concatenate([bf(x), bf(y)], axis=1)

    n = range(len(chains))
    gram = [_mm_nt(jnp.concatenate([c["q"], c["k"]], axis=0), c["k"]) for c in chains]
    yield
    decay = [jnp.exp(jnp.where(c["m"]["incl"], c["gc"] - c["gr"], -jnp.inf)) for c in chains]
    a = [jnp.where(chains[i]["m"]["strict"], gram[i][w:] * chains[i]["bc"] * decay[i], 0.0) for i in n]
    nd = [jnp.where(chains[i]["m"]["b16"], -a[i], 0.0) for i in n]
    n2 = [_mm(bf(nd[i]), bf(nd[i])) for i in n]
    yield
    p = [jnp.where(chains[i]["m"]["diag"], 1.0, nd[i]) for i in n]
    r = [_mm(bf(n2[i]), cat(p[i], n2[i])) for i in n]
    yield
    p = [p[i] + r[i][:, :w] for i in n]
    n4 = [r[i][:, w:] for i in n]
    r = [_mm(bf(n4[i]), cat(p[i], n4[i])) for i in n]
    yield
    p = [p[i] + r[i][:, :w] for i in n]
    n8 = [r[i][:, w:] for i in n]
    r = [_mm(bf(n8[i]), bf(p[i])) for i in n]
    yield
    di = [p[i] + r[i] for i in n]
    lo = [jnp.where(chains[i]["m"]["b16"], 0.0, a[i]) for i in n]
    nt = [-_mm(bf(di[i]), bf(lo[i])) for i in n]
    yield
    r = [_mm(bf(nt[i]), cat(nt[i], di[i])) for i in n]
    yield
    s1 = [di[i] + r[i][:, w:] for i in n]
    t = [s1[i] + _mm(bf(r[i][:, :w]), bf(s1[i])) for i in n]
    yield
    eg = [jnp.exp(c["gc"]) for c in chains]
    kf = [c["k"].astype(F32) for c in chains]
    rhs = [cat(kf[i] * (chains[i]["bc"] * eg[i]), chains[i]["v"].astype(F32) * chains[i]["bc"]) for i in n]
    wu = [bf(_mm(bf(t[i]), rhs[i])) for i in n]
    yield
    col_chunk = jnp.right_shift(lax.broadcasted_iota(jnp.int32, (GDN_DK, w), 1), 6)
    lhs = []
    for i in n:
        c = chains[i]
        qk = jnp.where(c["m"]["incl"], gram[i][:w] * decay[i], 0.0)
        k_end_t = jnp.transpose(kf[i] * jnp.exp(c["glc"] - c["gc"]))
        spread = [jnp.where(col_chunk == j, k_end_t, 0.0) for j in range(w // CHUNK)]
        lhs.append(bf(jnp.concatenate([qk] + spread, axis=0)))
    res = [_mm(lhs[i], wu[i]) for i in n]
    yield
    for i in n:
        c = chains[i]
        x = res[i][:w]
        qp = c["q"].astype(F32) * eg[i] - x[:, :GDN_DK]
        out.append((bf(qp), x[:, GDN_DK:], bf(res[i][w:])))


def _gdn_kernel(q_ref, k_ref, v_ref, gate_ref, col_ref, row_ref, ng_ref, o_ref,
                st_ref, of_ref, ob_ref, qp_ref, y_ref, *, S, hg):
    nwin = S // GDN_WIN
    cpw = GDN_WIN // CHUNK
    st_ref[...] = jnp.zeros(st_ref.shape, F32)

    def chains():
        for d in range(2):
            for hl in range(hg):
                yield d, hl, d * hg + hl

    def win_start(w, d):
        wi = w if d == 0 else nwin - 1 - w
        return wi, pl.multiple_of(wi * GDN_WIN, GDN_WIN)

    def prep_stages(w, slot):
        masks = (_window_masks(False), _window_masks(True))
        ins = []
        for d, hl, ch in chains():
            wi, r0 = win_start(w, d)
            hs = slice(hl * LANES, (hl + 1) * LANES)
            colv = col_ref[pl.ds(r0, GDN_WIN), :]
            rowv = row_ref[wi]
            gi, bi, li = ch, 2 * hg + ch, 4 * hg + ch
            ins.append(dict(q=q_ref[pl.ds(r0, GDN_WIN), hs], k=k_ref[pl.ds(r0, GDN_WIN), hs],
                            v=v_ref[pl.ds(r0, GDN_WIN), hs], gc=colv[:, gi:gi + 1], gr=rowv[gi:gi + 1, :],
                            bc=colv[:, bi:bi + 1], glc=colv[:, li:li + 1], m=masks[d]))
        out = []
        yield from _delta_prep_stages(ins, out)
        for (d, hl, ch), (qp, o0, y) in zip(chains(), out):
            _, r0 = win_start(w, d)
            qp_ref[slot, ch] = qp
            y_ref[slot, ch] = y
            (of_ref if d == 0 else ob_ref)[pl.ds(r0, GDN_WIN), pl.ds(hl * LANES, LANES)] = o0

    def seq_stages(w, slot):
        states = [st_ref[ch] for _, _, ch in chains()]
        for step in range(cpw):
            for i, (d, hl, ch) in enumerate(chains()):
                _, r0 = win_start(w, d)
                hs = slice(hl * LANES, (hl + 1) * LANES)
                dst = of_ref if d == 0 else ob_ref
                j = step if d == 0 else cpw - 1 - step
                rj = pl.multiple_of(r0 + j * CHUNK, CHUNK)
                yj = y_ref[slot, ch, j * GDN_DK:(j + 1) * GDN_DK, :]
                lhs = jnp.concatenate([yj[:, :GDN_DK], qp_ref[slot, ch, j * CHUNK:(j + 1) * CHUNK, :]], axis=0)
                r = _mm(lhs, states[i].astype(BF16))
                glv = col_ref[pl.ds(rj, 1), :][:, 4 * hg + ch:4 * hg + ch + 1]
                dst[pl.ds(rj, CHUNK), hs] = dst[pl.ds(rj, CHUNK), hs] + r[GDN_DK:]
                states[i] = states[i] * jnp.exp(glv) - r[:GDN_DK] + yj[:, GDN_DK:].astype(F32)
            yield
        for i, (_, _, ch) in enumerate(chains()):
            st_ref[ch] = states[i]

    def run(*gens):
        live = list(gens)
        while live:
            for g in list(live):
                try:
                    next(g)
                except StopIteration:
                    live.remove(g)

    run(prep_stages(0, 0))

    def body(w, carry):
        slot = lax.rem(w, 2)
        run(seq_stages(w, slot), prep_stages(w + 1, 1 - slot))
        return carry

    lax.fori_loop(0, nwin - 1, body, 0)
    run(seq_stages(nwin - 1, (nwin - 1) % 2))

    rows = 256
    for t in range(S // rows):
        rs = slice(t * rows, (t + 1) * rows)
        for hl in range(hg):
            hs = slice(hl * LANES, (hl + 1) * LANES)
            o = of_ref[rs, hs] + ob_ref[rs, hs]
            ms = jnp.mean(o * o, axis=-1, keepdims=True)
            gate = gate_ref[rs, hs].astype(F32)
            y = o * lax.rsqrt(ms + EPS) * ng_ref[...]
            o_ref[rs, hs] = (y * (gate * jax.nn.sigmoid(gate))).astype(o_ref.dtype)


def gdn_scan(qkvc, big, gate_col_block0, col, row, norm_g, *, B, S):
    T = B * S
    hg = GDN_HG
    ngroups = GDN_HEADS // hg
    w = hg * LANES
    nwin = S // GDN_WIN
    cpw = GDN_WIN // CHUNK
    assert S % GDN_WIN == 0 and nwin >= 2
    blocks = [5 * _nbytes((S, w), BF16), _nbytes((S, LANES), F32), _nbytes((nwin, 8, GDN_WIN), F32)]
    scratch = (_nbytes((2 * hg, GDN_DK, GDN_DV), F32) + 2 * _nbytes((S, w), F32)
               + _nbytes((2, 2 * hg, GDN_WIN, GDN_DK), BF16) + _nbytes((2, 2 * hg, cpw * GDN_DK, 2 * GDN_DV), BF16))
    return pl.pallas_call(
        functools.partial(_gdn_kernel, S=S, hg=hg),
        grid=(B, ngroups),
        in_specs=[
            pl.BlockSpec((S, w), lambda b, g: (b, g)),
            pl.BlockSpec((S, w), lambda b, g: (b, ngroups + g)),
            pl.BlockSpec((S, w), lambda b, g: (b, 2 * ngroups + g)),
            pl.BlockSpec((S, w), lambda b, g: (b, gate_col_block0 + g)),
            pl.BlockSpec((None, None, S, 6 * hg), lambda b, g: (b, g, 0, 0)),
            pl.BlockSpec((None, None, nwin, 2 * hg, GDN_WIN), lambda b, g: (b, g, 0, 0, 0)),
            pl.BlockSpec((1, GDN_DV), lambda b, g: (0, 0)),
        ],
        out_specs=pl.BlockSpec((S, w), lambda b, g: (b, g)),
        out_shape=jax.ShapeDtypeStruct((T, GDN_HEADS * GDN_DV), BF16),
        scratch_shapes=[
            pltpu.VMEM((2 * hg, GDN_DK, GDN_DV), F32),
            pltpu.VMEM((S, w), F32),
            pltpu.VMEM((S, w), F32),
            pltpu.VMEM((2, 2 * hg, GDN_WIN, GDN_DK), BF16),
            pltpu.VMEM((2, 2 * hg, cpw * GDN_DK, 2 * GDN_DV), BF16),
        ],
        compiler_params=_params(("parallel", "parallel"), blocks, scratch),
        name="gdn_scan",
    )(qkvc, qkvc, qkvc, big, col, row, norm_g)


def _pack_bf16_pair(a, b):
    hi = lax.bitcast_convert_type(a.astype(BF16).astype(F32), jnp.uint32)
    lo = lax.bitcast_convert_type(b.astype(BF16).astype(F32), jnp.uint32)
    return hi | lax.shift_right_logical(lo, jnp.uint32(16))


def _unpack_bf16_pair(w):
    a = lax.bitcast_convert_type(w & jnp.uint32(0xFFFF0000), F32)
    b = lax.bitcast_convert_type(lax.shift_left(w, jnp.uint32(16)), F32)
    return a, b


def _router_kernel(h_ref, g_ref, rw_ref, rb_ref, n_ref, idx_ref, gate_ref):
    x = h_ref[...]
    ms = jnp.mean(x * x, axis=-1, keepdims=True)
    n = x * lax.rsqrt(ms + EPS) * g_ref[...]
    half = n.shape[1] // 2
    n_ref[...] = _pack_bf16_pair(n[:, :half], n[:, half:])
    logits = jnp.dot(n, rw_ref[...], precision=HIGHEST, preferred_element_type=F32) + rb_ref[...]
    lane = lax.broadcasted_iota(jnp.int32, logits.shape, 1).astype(F32)
    neg = -jnp.inf
    l = jnp.where(lane < N_EXPERTS, logits, neg)
    vals, idxs = [], []
    for _ in range(TOP_K):
        m = jnp.max(l, axis=-1, keepdims=True)
        idx = jnp.min(jnp.where(l == m, lane, float(LANES)), axis=-1, keepdims=True)
        vals.append(m)
        idxs.append(idx)
        l = jnp.where(lane == idx, neg, l)
    es = [jnp.exp(v - vals[0]) for v in vals]
    tot = es[0] + es[1] + es[2] + es[3]
    iout = jnp.zeros(logits.shape, F32)
    gout = jnp.zeros(logits.shape, F32)
    for k in range(TOP_K):
        iout = jnp.where(lane == k, idxs[k], iout)
        gout = jnp.where(lane == k, es[k] / tot, gout)
    idx_ref[...] = iout.astype(jnp.int32)
    gate_ref[...] = gout


def moe_router(h, g, rw, rb, *, tm):
    T, D = h.shape
    blocks = [2 * _nbytes((tm, D), F32), _nbytes((D, LANES), F32), 2 * _nbytes((tm, LANES), F32)]
    return pl.pallas_call(
        _router_kernel,
        grid=(T // tm,),
        in_specs=[
            pl.BlockSpec((tm, D), lambda i: (i, 0)),
            pl.BlockSpec((1, D), lambda i: (0, 0)),
            pl.BlockSpec((D, LANES), lambda i: (0, 0)),
            pl.BlockSpec((1, LANES), lambda i: (0, 0)),
        ],
        out_specs=[
            pl.BlockSpec((tm, D // 2), lambda i: (i, 0)),
            pl.BlockSpec((tm, LANES), lambda i: (i, 0)),
            pl.BlockSpec((tm, LANES), lambda i: (i, 0)),
        ],
        out_shape=[
            jax.ShapeDtypeStruct((T, D // 2), jnp.uint32),
            jax.ShapeDtypeStruct((T, LANES), jnp.int32),
            jax.ShapeDtypeStruct((T, LANES), F32),
        ],
        compiler_params=_params(("parallel",), blocks, 4 * _nbytes((tm, D), F32)),
        name="moe_router",
    )(h, g, rw, rb)


def _row_copy(src_hbm, row, dst_ref, dst_row, sem):
    return pltpu.make_async_copy(src_hbm.at[pl.ds(row, 1), :], dst_ref.at[pl.ds(dst_row, 1), :], sem)


def _rank_kernel(idx_ref, rank_ref, cnt_ref, carry_ref, *, tb):
    @pl.when(pl.program_id(0) == 0)
    def _():
        carry_ref[...] = jnp.zeros(carry_ref.shape, F32)

    idx = idx_ref[...]
    lane = lax.broadcasted_iota(jnp.int32, idx.shape, 1)
    hits = [lane == idx[:, k:k + 1] for k in range(TOP_K)]
    onehot = jnp.zeros(idx.shape, F32)
    for h in hits:
        onehot = onehot + jnp.where(h, 1.0, 0.0)
    ri = lax.broadcasted_iota(jnp.int32, (tb, tb), 0)
    ci = lax.broadcasted_iota(jnp.int32, (tb, tb), 1)
    tril = jnp.where(ri >= ci, 1.0, 0.0).astype(BF16)
    incl = _mm(tril, onehot.astype(BF16))
    carry = carry_ref[0:1, :]
    before = carry + incl - onehot
    out = jnp.zeros(idx.shape, F32)
    for k in range(TOP_K):
        rk = jnp.sum(jnp.where(hits[k], before, 0.0), axis=-1, keepdims=True)
        out = jnp.where(lane == k, rk, out)
    rank_ref[...] = out.astype(jnp.int32)
    total = jnp.broadcast_to(carry + incl[tb - 1:tb, :], carry_ref.shape)
    carry_ref[...] = total
    cnt_ref[...] = total


def moe_rank(top_idx, *, tb):
    T = top_idx.shape[0]
    blocks = [2 * _nbytes((tb, LANES), F32), _nbytes((8, LANES), F32)]
    return pl.pallas_call(
        functools.partial(_rank_kernel, tb=tb),
        grid=(T // tb,),
        in_specs=[pl.BlockSpec((tb, LANES), lambda i: (i, 0))],
        out_specs=[pl.BlockSpec((tb, LANES), lambda i: (i, 0)), pl.BlockSpec((8, LANES), lambda i: (0, 0))],
        out_shape=[jax.ShapeDtypeStruct((T, LANES), jnp.int32), jax.ShapeDtypeStruct((8, LANES), F32)],
        scratch_shapes=[pltpu.VMEM((8, LANES), F32)],
        compiler_params=_params(("arbitrary",), blocks, 4 * _nbytes((tb, tb), F32)),
        name="moe_rank",
    )(top_idx)


def _scatter_rows_kernel(dest_ref, zs_ref, nu_ref, src_ref, out_hbm, zero_ref, sem, zsem, *, tpb, bm, nb):
    i = pl.program_id(0)

    @pl.when(i == 0)
    def _():
        zero_ref[...] = jnp.zeros(zero_ref.shape, zero_ref.dtype)

        def zero_block(row0):
            return pltpu.make_async_copy(zero_ref, out_hbm.at[pl.ds(pl.multiple_of(row0, bm), bm), :], zsem)

        def start_unused(b, c):
            zero_block(b * bm).start()
            return c

        def wait_unused(b, c):
            zero_block(b * bm).wait()
            return c

        for e in range(N_EXPERTS):
            zero_block(zs_ref[e]).start()
        lax.fori_loop(nu_ref[0], nb, start_unused, 0)
        for e in range(N_EXPERTS):
            zero_block(zs_ref[e]).wait()
        lax.fori_loop(nu_ref[0], nb, wait_unused, 0)

    def issue(t, c):
        tok = i * tpb + t
        for k in range(TOP_K):
            pltpu.make_async_copy(src_ref.at[pl.ds(t, 1), :], out_hbm.at[pl.ds(dest_ref[tok * TOP_K + k], 1), :],
                                  sem).start()
        return c

    lax.fori_loop(0, tpb, issue, 0, unroll=4)

    for k in range(TOP_K):
        pltpu.make_async_copy(src_ref, out_hbm.at[pl.ds(0, tpb), :], sem).wait()


def moe_scatter_rows(dest, zero_start, n_used, src, *, rows, bm, tpb):
    T, W = src.shape
    blocks = [_nbytes((tpb, W), src.dtype)]
    return pl.pallas_call(
        functools.partial(_scatter_rows_kernel, tpb=tpb, bm=bm, nb=rows // bm),
        grid_spec=pltpu.PrefetchScalarGridSpec(
            num_scalar_prefetch=3,
            grid=(T // tpb,),
            in_specs=[pl.BlockSpec((tpb, W), lambda i, d, z, nu: (i, 0))],
            out_specs=pl.BlockSpec(memory_space=pl.ANY),
            scratch_shapes=[pltpu.VMEM((bm, W), src.dtype), pltpu.SemaphoreType.DMA, pltpu.SemaphoreType.DMA],
        ),
        out_shape=jax.ShapeDtypeStruct((rows, W), src.dtype),
        compiler_params=_params(("arbitrary",), blocks, _nbytes((bm, W), src.dtype)),
        name="moe_scatter_rows",
    )(dest, zero_start, n_used, src)


CAST_ROWS = 256


def _stream_expert_weights(plan, j, i, nj, copies, convert):
    be_ref, first_ref, next_e_ref, last_ref = plan

    @pl.when(jnp.logical_and(j == 0, i == 0))
    def _():
        for c in copies(be_ref[0], 0):
            c.start()

    @pl.when(first_ref[i] == 1)
    def _():
        for c in copies(be_ref[i], j):
            c.wait()
        convert()
        is_last = last_ref[i]

        @pl.when(jnp.logical_or(is_last == 0, j + 1 < nj))
        def _():
            for c in copies(next_e_ref[i], j + is_last):
                c.start()


def _cast_rows(src_ref, dst_ref, col0, width, rows):
    def body(r, c):
        sl = pl.ds(pl.multiple_of(r * CAST_ROWS, CAST_ROWS), CAST_ROWS)
        dst_ref[sl, col0:col0 + width] = src_ref[sl, :].astype(BF16)
        return c

    lax.fori_loop(0, rows // CAST_ROWS, body, 0)


def _expert_up_kernel(be_ref, first_ref, next_e_ref, last_ref, nu_ref, x_ref, w_hbm, bg_ref, bu_ref, o_ref,
                      stage_ref, wbf_ref, sem, *, tn, nj, dff, D):
    j = pl.program_id(0)
    i = pl.program_id(1)

    def copies(e, jj):
        return [pltpu.make_async_copy(w_hbm.at[e, :, pl.ds(pl.multiple_of(h * dff + jj * tn, tn), tn)],
                                      stage_ref.at[h], sem.at[h]) for h in range(2)]

    def convert():
        for h in range(2):
            _cast_rows(stage_ref.at[h], wbf_ref, h * tn, tn, D)

    _stream_expert_weights((be_ref, first_ref, next_e_ref, last_ref), j, i, nj, copies, convert)

    @pl.when(i < nu_ref[0])
    def _():
        xa, xb = _unpack_bf16_pair(x_ref[...])
        half = D // 2
        gu = _mm(xa.astype(BF16), wbf_ref[:half, :]) + _mm(xb.astype(BF16), wbf_ref[half:, :])
        gate = jnp.minimum(gu[:, :tn] + bg_ref[...], SWIGLU_LIMIT)
        up = jnp.clip(gu[:, tn:] + bu_ref[...], -SWIGLU_LIMIT, SWIGLU_LIMIT)
        act = (up + 1.0) * gate * jax.nn.sigmoid(SWIGLU_ALPHA * gate)
        o_ref[...] = act.astype(o_ref.dtype)

    @pl.when(i >= nu_ref[0])
    def _():
        o_ref[...] = jnp.zeros(o_ref.shape, o_ref.dtype)


def moe_expert_up(plan, n_used, xs, w_gu, b_gu, *, bm, tn):
    R = xs.shape[0]
    D = w_gu.shape[1]
    dff = w_gu.shape[2] // 2
    tn = min(tn, dff)
    nj = dff // tn
    nb = R // bm

    def last(i, nu):
        return jnp.minimum(i, nu[0] - 1)

    blocks = [_nbytes((bm, D), BF16), 2 * _nbytes((8, tn), F32), _nbytes((bm, tn), BF16)]
    scratch = _nbytes((2, D, tn), F32) + _nbytes((D, 2 * tn), BF16) + 4 * _nbytes((bm, tn), F32)
    return pl.pallas_call(
        functools.partial(_expert_up_kernel, tn=tn, nj=nj, dff=dff, D=D),
        grid_spec=pltpu.PrefetchScalarGridSpec(
            num_scalar_prefetch=5,
            grid=(nj, nb),
            in_specs=[
                pl.BlockSpec((bm, D // 2), lambda j, i, be, f, ne, lg, nu: (last(i, nu), 0)),
                pl.BlockSpec(memory_space=pl.ANY),
                pl.BlockSpec((None, 1, tn), lambda j, i, be, f, ne, lg, nu: (be[last(i, nu)], 0, j)),
                pl.BlockSpec((None, 1, tn), lambda j, i, be, f, ne, lg, nu: (be[last(i, nu)], 0, nj + j)),
            ],
            out_specs=pl.BlockSpec((bm, tn), lambda j, i, be, f, ne, lg, nu: (i, j)),
            scratch_shapes=[pltpu.VMEM((2, D, tn), F32), pltpu.VMEM((D, 2 * tn), BF16), pltpu.SemaphoreType.DMA((2,))],
        ),
        out_shape=jax.ShapeDtypeStruct((R, dff), BF16),
        compiler_params=_params(("arbitrary", "arbitrary"), blocks, scratch),
        name="moe_expert_up",
    )(*plan, n_used, xs, w_gu, b_gu, b_gu)


def _expert_down_kernel(be_ref, first_ref, next_e_ref, last_ref, nu_ref, a_ref, w_hbm, b_ref, o_ref,
                        stage_ref, wbf_ref, sem, *, tn, nj, dff):
    j = pl.program_id(0)
    i = pl.program_id(1)

    def copies(e, jj):
        return [pltpu.make_async_copy(w_hbm.at[e, :, pl.ds(pl.multiple_of(jj * tn, tn), tn)], stage_ref, sem.at[0])]

    def convert():
        _cast_rows(stage_ref, wbf_ref, 0, tn, dff)

    _stream_expert_weights((be_ref, first_ref, next_e_ref, last_ref), j, i, nj, copies, convert)

    @pl.when(i < nu_ref[0])
    def _():
        y = _mm(a_ref[...], wbf_ref[...]) + b_ref[...]
        o_ref[...] = _pack_bf16_pair(y[:, :tn // 2], y[:, tn // 2:])

    @pl.when(i >= nu_ref[0])
    def _():
        o_ref[...] = jnp.zeros(o_ref.shape, o_ref.dtype)


def moe_expert_down(plan, n_used, act, w_down, b_down, *, bm, tn):
    R, dff = act.shape
    D = w_down.shape[2]
    tn = min(tn, D)
    nj = D // tn

    def last(i, nu):
        return jnp.minimum(i, nu[0] - 1)

    blocks = [_nbytes((bm, dff), BF16), _nbytes((8, tn), F32), _nbytes((bm, tn), F32)]
    scratch = _nbytes((dff, tn), F32) + _nbytes((dff, tn), BF16) + _nbytes((bm, tn), F32)
    return pl.pallas_call(
        functools.partial(_expert_down_kernel, tn=tn, nj=nj, dff=dff),
        grid_spec=pltpu.PrefetchScalarGridSpec(
            num_scalar_prefetch=5,
            grid=(nj, R // bm),
            in_specs=[
                pl.BlockSpec((bm, dff), lambda j, i, be, f, ne, lg, nu: (last(i, nu), 0)),
                pl.BlockSpec(memory_space=pl.ANY),
                pl.BlockSpec((None, 1, tn), lambda j, i, be, f, ne, lg, nu: (be[last(i, nu)], 0, j)),
            ],
            out_specs=pl.BlockSpec((bm, tn // 2), lambda j, i, be, f, ne, lg, nu: (i, j)),
            scratch_shapes=[pltpu.VMEM((dff, tn), F32), pltpu.VMEM((dff, tn), BF16), pltpu.SemaphoreType.DMA((1,))],
        ),
        out_shape=jax.ShapeDtypeStruct((R, D // 2), jnp.uint32),
        compiler_params=_params(("arbitrary", "arbitrary"), blocks, scratch),
        name="moe_expert_down",
    )(*plan, n_used, act, w_down, b_down)


def _combine_kernel(dest_ref, h_ref, gt_ref, y_hbm, g_ref, o_ref, buf_ref, sem, *, tb, nblocks, pair_tile):
    i = pl.program_id(0)

    def start_block(blk):
        slot = lax.rem(blk, 2)

        def issue(r, c):
            for k in range(TOP_K):
                _row_copy(y_hbm, dest_ref[(blk * tb + r) * TOP_K + k], buf_ref.at[slot, k], r, sem.at[slot]).start()
            return c

        lax.fori_loop(0, tb, issue, 0, unroll=4)

    @pl.when(i == 0)
    def _():
        start_block(i)

    @pl.when(i + 1 < nblocks)
    def _():
        start_block(i + 1)

    slot = lax.rem(i, 2)
    for k in range(TOP_K):
        pltpu.make_async_copy(y_hbm.at[pl.ds(0, tb), :], buf_ref.at[slot, k], sem.at[slot]).wait()

    def body(r, c):
        sl = pl.ds(pl.multiple_of(r * NORM_ROWS, NORM_ROWS), NORM_ROWS)
        gt = gt_ref[sl, :]
        acc_a = acc_b = None
        for k in range(TOP_K):
            a, b = _unpack_bf16_pair(buf_ref[slot, k, sl, :])
            gk = gt[:, k:k + 1]
            acc_a = a * gk if acc_a is None else acc_a + a * gk
            acc_b = b * gk if acc_b is None else acc_b + b * gk
        pieces = []
        for j in range(acc_a.shape[1] // pair_tile):
            cs = slice(j * pair_tile, (j + 1) * pair_tile)
            pieces += [acc_a[:, cs], acc_b[:, cs]]
        hv = h_ref[sl, :] + jnp.concatenate(pieces, axis=1)
        ms = jnp.mean(hv * hv, axis=-1, keepdims=True)
        o_ref[sl, :] = hv * lax.rsqrt(ms + EPS) * g_ref[...]
        return c

    lax.fori_loop(0, tb // NORM_ROWS, body, 0)


def moe_combine(dest, h, top_gate, y, g, *, tb, pair_tile):
    T, D = h.shape
    blocks = [2 * _nbytes((tb, D), F32), _nbytes((tb, LANES), F32)]
    return pl.pallas_call(
        functools.partial(_combine_kernel, tb=tb, nblocks=T // tb, pair_tile=pair_tile),
        grid_spec=pltpu.PrefetchScalarGridSpec(
            num_scalar_prefetch=1,
            grid=(T // tb,),
            in_specs=[
                pl.BlockSpec((tb, D), lambda i, d: (i, 0)),
                pl.BlockSpec((tb, LANES), lambda i, d: (i, 0)),
                pl.BlockSpec(memory_space=pl.ANY),
                pl.BlockSpec((1, D), lambda i, d: (0, 0)),
            ],
            out_specs=pl.BlockSpec((tb, D), lambda i, d: (i, 0)),
            scratch_shapes=[pltpu.VMEM((2, TOP_K, tb, D // 2), jnp.uint32), pltpu.SemaphoreType.DMA((2,))],
        ),
        out_shape=jax.ShapeDtypeStruct((T, D), F32),
        compiler_params=_params(("arbitrary",), blocks, _nbytes((2, TOP_K, tb, D), F32)),
        name="moe_combine",
    )(dest, h, top_gate, y, g)


def _routing_tables(top_idx, rank, counts, *, bm):
    T = top_idx.shape[0]
    tk = T * TOP_K
    flat_e = top_idx.reshape(tk)
    padded = (counts + bm - 1) // bm * bm
    pad_end = jnp.cumsum(padded)
    pad_start = pad_end - padded
    dest = (pad_start[flat_e] + rank.reshape(tk)).astype(jnp.int32)
    nb = tk // bm + N_EXPERTS
    rows = nb * bm
    zero_start = jnp.maximum(pad_end - bm, 0).astype(jnp.int32)
    block_row0 = jnp.arange(nb, dtype=pad_end.dtype) * bm
    block_expert = jnp.minimum(
        jnp.sum((pad_end[None, :] <= block_row0[:, None]).astype(jnp.int32), axis=1), N_EXPERTS - 1
    ).astype(jnp.int32)
    n_used = (pad_end[-1] // bm).astype(jnp.int32).reshape(1)
    blk = jnp.arange(nb, dtype=jnp.int32)
    used = blk < n_used[0]
    be_used = jnp.where(used, block_expert, N_EXPERTS)
    prev = jnp.concatenate([jnp.full((1,), -1, jnp.int32), be_used[:-1]])
    first = jnp.logical_and(used, be_used != prev).astype(jnp.int32)
    nxt = jnp.sum((be_used[None, :] <= be_used[:, None]).astype(jnp.int32), axis=1)
    is_last = (nxt >= n_used[0]).astype(jnp.int32)
    next_e = jnp.where(is_last == 1, block_expert[0], block_expert[jnp.minimum(nxt, nb - 1)]).astype(jnp.int32)
    plan = (block_expert, first, next_e, is_last)
    return dest, zero_start, plan, n_used, rows


def _rot_half(w):
    half = QK_ROPE // 2
    return jnp.concatenate([-w[..., half:], w[..., :half]], axis=-1)


def _layer(h, pos, invf, ln1_g, w_in, q_norm_g, w_q_up, kv_norm_g, w_kv_up, conv_w, a_log, dt_bias, gdn_norm_g,
           w_mla_o, w_gdn_o, w_out, ln2_g, router_w, router_b, w_gu, b_gu, w_down, b_down, out_g, *, B, S):
    T, D = h.shape
    H = MLA_HEADS
    gqkv = 2 * GDN_HEADS * GDN_DK + GDN_HEADS * GDN_DV
    gw = GDN_HEADS * GDN_DV
    o0 = Q_LORA + KV_LORA
    c_kr, c_qkv = o0, o0 + QK_ROPE
    c_gate = c_qkv + gqkv
    c_ab = c_gate + gw
    c_za = c_ab + 4 * GDN_HEADS
    c_zb = c_za + D

    w_kr = w_in[:, c_kr:c_qkv]
    w_kt = _rot_half(w_kr)
    small_cols = [w_in[:, :o0], w_kr, w_kr, w_kt, w_kt, w_in[:, c_ab:c_za]]
    n_small = 2048
    used = o0 + 4 * QK_ROPE + 4 * GDN_HEADS
    small_cols.append(jnp.zeros((D, n_small - used), F32))
    w_small = jnp.concatenate(small_cols, axis=1).astype(BF16)
    w_big = jnp.concatenate([w_in[:, c_qkv:c_gate], w_in[:, c_gate:c_ab], w_in[:, c_za:c_zb], w_in[:, c_zb:]],
                            axis=1).astype(BF16)
    ka_blk = o0 // LANES
    kb_blk = ka_blk + 1
    ab_blk = ka_blk + 2

    wq = w_q_up.reshape(Q_LORA, H, QK_NOPE + QK_ROPE)
    wq_rope = wq[:, :, QK_NOPE:]
    wq2 = jnp.concatenate([wq[:, :, :QK_NOPE], wq_rope, _rot_half(wq_rope)], axis=-1).reshape(Q_LORA, H * 2 * LANES)
    wq2 = wq2.astype(BF16)
    wkv = w_kv_up.reshape(KV_LORA, H, QK_NOPE + V_HEAD)
    wkv2 = jnp.concatenate([wkv[:, :, :QK_NOPE].reshape(KV_LORA, H * QK_NOPE),
                            wkv[:, :, QK_NOPE:].reshape(KV_LORA, H * V_HEAD)], axis=1).astype(BF16)

    n1 = rmsnorm_cast(h, ln1_g, tm=512)
    small = matmul(n1, w_small, out_dtype=F32, tm=1024, tn=1024, name="in_proj_small")
    big = matmul(n1, w_big, out_dtype=BF16, tm=1024, tn=1024, name="in_proj_big")

    q2 = norm_matmul(small, q_norm_g, wq2, out_dtype=BF16, tm=1024, tn=1024, x_col_block=0, name="q_up")
    kv = norm_matmul(small, kv_norm_g, wkv2, out_dtype=BF16, tm=1024, tn=1024, x_col_block=Q_LORA // KV_LORA,
                     name="kv_up")
    score_scale = (QK_NOPE + QK_ROPE) ** -0.5 * LOG2_E
    kr2, cs = rope_prep(pos, invf, small, ka_blk, kb_blk, tm=1024, scale=score_scale)
    o_a = mla_attention(q2, cs, kv, kr2, B=B, S=S, tq=256, scale=score_scale)

    qkvc = gdn_conv(big, conv_w, B=B, S=S)
    pad = jnp.zeros((LANES - 2 * GDN_HEADS,), F32)
    alog_l = jnp.concatenate([a_log.reshape(-1), pad]).reshape(1, LANES)
    dtb_l = jnp.concatenate([dt_bias.reshape(-1), pad]).reshape(1, LANES)
    gp = gdn_gates(small, ab_blk, alog_l, dtb_l, tm=512)
    hg = GDN_HG
    ng = GDN_HEADS // hg
    nch = S // CHUNK
    nwin = S // GDN_WIN
    gp4 = gp[:, :4 * GDN_HEADS].reshape(B, S, 4, ng, hg)
    gcum = gp4[:, :, :2].reshape(B, nch, CHUNK, 2, ng, hg)
    g_end = jnp.stack([gcum[:, :, CHUNK - 1, 0], gcum[:, :, 0, 1]], axis=2)
    g_end = jnp.broadcast_to(g_end[:, :, None], (B, nch, CHUNK, 2, ng, hg)).reshape(B, S, 2, ng, hg)
    col = jnp.concatenate([gp4, g_end], axis=2).transpose(0, 3, 1, 2, 4).reshape(B, ng, S, 6 * hg)
    row = gp4[:, :, :2].reshape(B, nwin, GDN_WIN, 2, ng, hg).transpose(0, 4, 1, 3, 5, 2)
    row = row.reshape(B, ng, nwin, 2 * hg, GDN_WIN)
    o_b = gdn_scan(qkvc, big, gqkv // (hg * LANES), col, row, gdn_norm_g.reshape(1, GDN_DV), B=B, S=S)

    merged = gated_pair_matmul(o_a, w_mla_o.astype(BF16), o_b, w_gdn_o.astype(BF16), big,
                               (gqkv + gw) // 1024, (gqkv + gw + D) // 1024, tm=512, tn=1024, name="mixer_out")
    h1 = residual_matmul(merged, w_out.astype(BF16), h, tm=512, tn=1024, name="out_proj")

    rw = jnp.concatenate([router_w, jnp.zeros((D, LANES - N_EXPERTS), F32)], axis=1)
    rb = jnp.concatenate([router_b, jnp.zeros((LANES - N_EXPERTS,), F32)]).reshape(1, LANES)
    n2, top_idx, top_gate = moe_router(h1, ln2_g, rw, rb, tm=256)
    rank, counts = moe_rank(top_idx, tb=512)
    dest, zero_start, plan, n_used, rows = _routing_tables(
        top_idx[:, :TOP_K], rank[:, :TOP_K], counts[0, :N_EXPERTS].astype(jnp.int32), bm=MOE_BM)
    xs = moe_scatter_rows(dest, zero_start, n_used, n2, rows=rows, bm=MOE_BM, tpb=512)
    dff = w_gu.shape[2] // 2
    act = moe_expert_up(plan, n_used, xs, w_gu, b_gu.reshape(N_EXPERTS, 1, 2 * dff), bm=MOE_BM, tn=512)
    down_tn = min(2048, D)
    y = moe_expert_down(plan, n_used, act, w_down, b_down.reshape(N_EXPERTS, 1, D), bm=MOE_BM, tn=down_tn)
    return moe_combine(dest, h1, top_gate, y, out_g, tb=128, pair_tile=down_tn // 2)


def kernel(x, positions, ln1_g, w_in, q_norm_g, w_q_up, kv_norm_g, w_kv_up, gdn_conv_w, gdn_a_log, gdn_dt_bias,
           gdn_norm_g, w_mla_o, w_gdn_o, w_out, ln2_g, router_w, router_b, w_gu, b_gu, w_down, b_down, ln_f_g):
    B, S, D = x.shape
    depth = ln1_g.shape[0]
    assert depth == 1, "the final norm is fused into the last layer's expert combine"
    half = QK_ROPE // 2
    inv_freq = ROPE_THETA ** (-jnp.arange(0, QK_ROPE, 2, dtype=F32) / QK_ROPE)
    invf = jnp.tile(inv_freq, LANES // half).reshape(1, LANES)
    pos = positions.reshape(B * S, 1).astype(jnp.int32)
    h = x.reshape(B * S, D)
    l = 0
    out = _layer(h, pos, invf, ln1_g[l].reshape(1, D), w_in[l], q_norm_g[l].reshape(1, Q_LORA), w_q_up[l],
                 kv_norm_g[l].reshape(1, KV_LORA), w_kv_up[l], gdn_conv_w[l], gdn_a_log[l], gdn_dt_bias[l],
                 gdn_norm_g[l], w_mla_o[l], w_gdn_o[l], w_out[l], ln2_g[l].reshape(1, D), router_w[l], router_b[l],
                 w_gu[l], b_gu[l], w_down[l], b_down[l], ln_f_g.reshape(1, D), B=B, S=S)
    return out.reshape(B, S, D)
```
